```python
import jax
import jax.numpy as jnp
from jax import lax
import numpy as np

D_MODEL = 4096
BATCH = 2
SEQ = 8192
DEPTH = 1

CHUNK = 64
MIX_WIDTH = D_MODEL
GMLP_WIDTH = MIX_WIDTH // 2
GMLP_HEADS = 16
GMLP_HEAD_CH = GMLP_WIDTH // GMLP_HEADS
GMLP_BLOCK = 128
RWKV_WIDTH = MIX_WIDTH - GMLP_WIDTH
RWKV_HEAD_DIM = 64
RWKV_HEADS = RWKV_WIDTH // RWKV_HEAD_DIM
DECAY_LORA = 64
ICLR_LORA = 64
GATE_LORA = 256
SHIFT_WIDTH = 3 * RWKV_WIDTH + DECAY_LORA + ICLR_LORA + GATE_LORA
PROJ_WIDTH = 2 * GMLP_WIDTH + SHIFT_WIDTH
D_FF = 4 * D_MODEL
RMS_EPS = 1e-6
LN_EPS = 1e-5
GN_EPS = RWKV_HEAD_DIM * 1e-5
L2_EPS = 1e-12

kernel_name = 'hybrid_gmlp_rwkv7_sandwich_block'


def rms_norm(x, g):
    xf = x.astype(jnp.float32)
    y = xf * lax.rsqrt(jnp.mean(xf * xf, axis=-1, keepdims=True) + RMS_EPS)
    return (y * g.astype(jnp.float32)).astype(x.dtype)


def layer_norm(x, g, b):
    xf = x.astype(jnp.float32)
    xc = xf - jnp.mean(xf, axis=-1, keepdims=True)
    y = xc * lax.rsqrt(jnp.mean(xc * xc, axis=-1, keepdims=True) + LN_EPS)
    return (y * g.astype(jnp.float32) + b.astype(jnp.float32)).astype(x.dtype)


def block_causal_mask():
    pos = jnp.arange(GMLP_BLOCK)
    return (pos[None, :] // CHUNK) <= (pos[:, None] // CHUNK)


def gmlp_spatial_gating(p_g, ln_g, ln_b, ws, bs):
    B, T, _ = p_g.shape
    z = jax.nn.gelu(p_g)
    u, v = z[..., :GMLP_WIDTH], z[..., GMLP_WIDTH:]
    v = layer_norm(v, ln_g, ln_b)
    vb = v.reshape(B, T // GMLP_BLOCK, GMLP_BLOCK, GMLP_HEADS, GMLP_HEAD_CH)
    ws_m = jnp.where(block_causal_mask()[None], ws, jnp.zeros((), ws.dtype))
    mixed = jnp.einsum('hij,bnjhc->bnihc', ws_m, vb) + bs.T[:, :, None]
    return u * mixed.reshape(B, T, GMLP_WIDTH)


def rwkv7_scan(r, w, k, v, kk, a):
    B, T, H, N = r.shape

    def step(S, inp):
        r_t, w_t, k_t, v_t, kk_t, a_t = inp
        sa = jnp.einsum('bhvk,bhk->bhv', S, -kk_t)
        S = (S * w_t[:, :, None, :]
             + sa[..., None] * (kk_t * a_t)[:, :, None, :]
             + v_t[..., None] * k_t[:, :, None, :])
        return S, jnp.einsum('bhvk,bhk->bhv', S, r_t)

    xs = tuple(jnp.moveaxis(t.astype(jnp.float32), 1, 0) for t in (r, w, k, v, kk, a))
    S0 = jnp.zeros((B, H, N, N), jnp.float32)
    _, y = lax.scan(step, S0, xs)
    return jnp.moveaxis(y, 0, 1)


def rwkv7_time_mix(p_s, mu, w0, w_up, a0, a_up, g_up, k_k, k_a, r_k, lnx_g, lnx_b):
    B, T, _ = p_s.shape
    H, N = RWKV_HEADS, RWKV_HEAD_DIM
    p_prev = jnp.pad(p_s[:, :-1], ((0, 0), (1, 0), (0, 0)))
    p_s = p_s + (p_prev - p_s) * mu
    cuts = [RWKV_WIDTH, 2 * RWKV_WIDTH, 3 * RWKV_WIDTH,
            3 * RWKV_WIDTH + DECAY_LORA, 3 * RWKV_WIDTH + DECAY_LORA + ICLR_LORA]
    r, k, v, xw, xa, xg = jnp.split(p_s, cuts, axis=-1)
    w_log = -jax.nn.softplus(-(w0 + jnp.tanh(xw) @ w_up)) - 0.5
    decay = jnp.exp(-jnp.exp(w_log.astype(jnp.float32)))
    a = jax.nn.sigmoid(a0 + xa @ a_up)
    g = jax.nn.sigmoid(xg) @ g_up
    r, k, v, decay, a = (t.reshape(B, T, H, N) for t in (r, k, v, decay, a))
    kk = k.astype(jnp.float32) * k_k.reshape(H, N).astype(jnp.float32)
    kk = kk / jnp.maximum(jnp.sqrt(jnp.sum(kk * kk, axis=-1, keepdims=True)), L2_EPS)
    k = k * (1 + (a - 1) * k_a.reshape(H, N))
    y = rwkv7_scan(r, decay, k, v, kk, a)
    yc = y - jnp.mean(y, axis=-1, keepdims=True)
    y = yc * lax.rsqrt(jnp.mean(yc * yc, axis=-1, keepdims=True) + GN_EPS)
    y = y * lnx_g.reshape(H, N).astype(jnp.float32) + lnx_b.reshape(H, N).astype(jnp.float32)
    bonus = jnp.sum((r * k * r_k).astype(jnp.float32), axis=-1, keepdims=True)
    y = y + bonus * v.astype(jnp.float32)
    return (y.reshape(B, T, RWKV_WIDTH) * g.astype(jnp.float32)).astype(p_s.dtype)


def setup_inputs(seed: int = 0) -> dict:
    key = jax.random.key(seed)
    ks = jax.random.split(key, 24)
    f32 = jnp.float32
    L = DEPTH

    def nrm(k, shape, s):
        return s * jax.random.normal(k, shape, f32)

    return {
        'x': jax.random.normal(ks[0], (BATCH, SEQ, D_MODEL), f32),
        'pre_mix_g': 1.0 + nrm(ks[1], (L, D_MODEL), 0.05),
        'w_in': nrm(ks[2], (L, D_MODEL, PROJ_WIDTH), D_MODEL ** -0.5),
        'tshift_mu': jax.random.uniform(ks[3], (L, SHIFT_WIDTH), f32),
        'gmlp_ln_g': 1.0 + nrm(ks[4], (L, GMLP_WIDTH), 0.05),
        'gmlp_ln_b': nrm(ks[5], (L, GMLP_WIDTH), 0.02),
        'gmlp_ws': nrm(ks[6], (L, GMLP_HEADS, GMLP_BLOCK, GMLP_BLOCK), GMLP_BLOCK ** -0.5),
        'gmlp_bs': 1.0 + nrm(ks[7], (L, GMLP_HEADS, GMLP_BLOCK), 0.1),
        'decay_w0': 0.5 + nrm(ks[8], (L, RWKV_WIDTH), 1.0),
        'decay_up': nrm(ks[9], (L, DECAY_LORA, RWKV_WIDTH), 0.5 * DECAY_LORA ** -0.5),
        'iclr_a0': nrm(ks[10], (L, RWKV_WIDTH), 0.5),
        'iclr_up': nrm(ks[11], (L, ICLR_LORA, RWKV_WIDTH), ICLR_LORA ** -0.5),
        'gate_up': nrm(ks[12], (L, GATE_LORA, RWKV_WIDTH), GATE_LORA ** -0.5),
        'k_k': 0.85 + nrm(ks[13], (L, RWKV_WIDTH), 0.05),
        'k_a': 1.0 + nrm(ks[14], (L, RWKV_WIDTH), 0.05),
        'r_k': nrm(ks[15], (L, RWKV_HEADS, RWKV_HEAD_DIM), 0.1),
        'lnx_g': 1.0 + nrm(ks[16], (L, RWKV_WIDTH), 0.05),
        'lnx_b': nrm(ks[17], (L, RWKV_WIDTH), 0.02),
        'w_out': nrm(ks[18], (L, MIX_WIDTH, D_MODEL), MIX_WIDTH ** -0.5),
        'post_mix_g': 1.0 + nrm(ks[19], (L, D_MODEL), 0.05),
        'pre_ffn_g': 1.0 + nrm(ks[20], (L, D_MODEL), 0.05),
        'w_ff1': nrm(ks[21], (L, D_MODEL, D_FF), D_MODEL ** -0.5),
        'w_ff2': nrm(ks[22], (L, D_FF, D_MODEL), D_FF ** -0.5),
        'post_ffn_g': 1.0 + nrm(ks[23], (L, D_MODEL), 0.05),
    }


def reference(x, pre_mix_g, w_in, tshift_mu, gmlp_ln_g, gmlp_ln_b, gmlp_ws, gmlp_bs,
              decay_w0, decay_up, iclr_a0, iclr_up, gate_up, k_k, k_a, r_k, lnx_g, lnx_b,
              w_out, post_mix_g, pre_ffn_g, w_ff1, w_ff2, post_ffn_g):
    for l in range(DEPTH):
        h = rms_norm(x, pre_mix_g[l])
        p = h @ w_in[l]
        y_a = gmlp_spatial_gating(p[..., :2 * GMLP_WIDTH], gmlp_ln_g[l], gmlp_ln_b[l],
                                  gmlp_ws[l], gmlp_bs[l])
        y_b = rwkv7_time_mix(p[..., 2 * GMLP_WIDTH:], tshift_mu[l], decay_w0[l], decay_up[l],
                             iclr_a0[l], iclr_up[l], gate_up[l], k_k[l], k_a[l], r_k[l],
                             lnx_g[l], lnx_b[l])
        mix = jnp.concatenate([y_a, y_b], axis=-1) @ w_out[l]
        x = x + rms_norm(mix, post_mix_g[l])
        h = rms_norm(x, pre_ffn_g[l])
        f = jnp.square(jax.nn.relu(h @ w_ff1[l])) @ w_ff2[l]
        x = x + rms_norm(f, post_ffn_g[l])
    return x
```

```python
import functools

import jax
import jax.numpy as jnp
from jax import lax
from jax.experimental import pallas as pl
from jax.experimental.pallas import tpu as pltpu

F32 = jnp.float32
BF16 = jnp.bfloat16
HIGHEST = lax.Precision.HIGHEST

RMS_EPS = 1e-6
LN_EPS = 1e-5
L2_EPS = 1e-12
GN_EPS_PER_CH = 1e-5

V7X_LANES = 128
RWKV_HEAD_DIM = 64
GMLP_BLOCK = 128
STREAM_CHUNK = 64
SCAN_CHUNK = 64
VMEM_LIMIT = 56 * 1024 * 1024


def _cparams(*sem):
    return pltpu.CompilerParams(dimension_semantics=sem, vmem_limit_bytes=VMEM_LIMIT)


def _nt(a, b, precision=None):
    return lax.dot_general(a, b, (((1,), (1,)), ((), ())), precision=precision,
                           preferred_element_type=F32)


def _tn(a, b, precision=None):
    return lax.dot_general(a, b, (((0,), (0,)), ((), ())), precision=precision,
                           preferred_element_type=F32)


def _mm(a, b, precision=None):
    return jnp.dot(a, b, precision=precision, preferred_element_type=F32)


def _head_ones(n=V7X_LANES, head=RWKV_HEAD_DIM):
    r = lax.broadcasted_iota(jnp.int32, (n, n), 0) // head
    c = lax.broadcasted_iota(jnp.int32, (n, n), 1) // head
    return (r == c).astype(F32)


def _head_sum(x, ones_bd):
    w = x.shape[1]
    parts = [_mm(x[:, j:j + V7X_LANES], ones_bd, HIGHEST) for j in range(0, w, V7X_LANES)]
    return parts[0] if len(parts) == 1 else jnp.concatenate(parts, axis=1)


def _rms_kernel(x_ref, g_ref, o_ref):
    x = x_ref[...]
    ms = jnp.mean(x * x, axis=-1, keepdims=True)
    o_ref[...] = (x * lax.rsqrt(ms + RMS_EPS) * g_ref[...]).astype(o_ref.dtype)


def rms_norm_bf16(x, g, tm):
    n, d = x.shape
    return pl.pallas_call(
        _rms_kernel,
        grid=(n // tm,),
        in_specs=[pl.BlockSpec((tm, d), lambda i: (i, 0)), pl.BlockSpec((1, d), lambda i: (0, 0))],
        out_specs=pl.BlockSpec((tm, d), lambda i: (i, 0)),
        out_shape=jax.ShapeDtypeStruct((n, d), BF16),
        compiler_params=_cparams("parallel"),
        name="rms_norm",
    )(x, g.reshape(1, d))


def _mm_kernel(a_ref, w_ref, o_ref):
    o_ref[...] = _mm(a_ref[...], w_ref[...]).astype(o_ref.dtype)


def matmul(a, w, tm, tn, name):
    m, k = a.shape
    n = w.shape[1]
    return pl.pallas_call(
        _mm_kernel,
        grid=(m // tm, n // tn),
        in_specs=[pl.BlockSpec((tm, k), lambda i, j: (i, 0)), pl.BlockSpec((k, tn), lambda i, j: (0, j))],
        out_specs=pl.BlockSpec((tm, tn), lambda i, j: (i, j)),
        out_shape=jax.ShapeDtypeStruct((m, n), F32),
        compiler_params=_cparams("parallel", "arbitrary"),
        name=name,
    )(a, w)


def _mm2_kernel(a1_ref, a2_ref, w1_ref, w2_ref, o_ref):
    o_ref[...] = _mm(a1_ref[...], w1_ref[...]) + _mm(a2_ref[...], w2_ref[...])


def matmul_concat2(a1, a2, w, tm, tn, name):
    m, k1 = a1.shape
    k2 = a2.shape[1]
    n = w.shape[1]
    nb1 = 1
    assert k1 == k2
    return pl.pallas_call(
        _mm2_kernel,
        grid=(m // tm, n // tn),
        in_specs=[pl.BlockSpec((tm, k1), lambda i, j: (i, 0)),
                  pl.BlockSpec((tm, k2), lambda i, j: (i, 0)),
                  pl.BlockSpec((k1, tn), lambda i, j: (0, j)),
                  pl.BlockSpec((k2, tn), lambda i, j: (nb1, j))],
        out_specs=pl.BlockSpec((tm, tn), lambda i, j: (i, j)),
        out_shape=jax.ShapeDtypeStruct((m, n), F32),
        compiler_params=_cparams("parallel", "arbitrary"),
        name=name,
    )(a1, a2, w, w)


def _gmlp_kernel(u_ref, v_ref, lng_ref, lnb_ref, ws_ref, bs_ref, o_ref, *, heads, nblk):
    zv = jax.nn.gelu(v_ref[...])
    mean = jnp.mean(zv, axis=-1, keepdims=True)
    xc = zv - mean
    var = jnp.mean(xc * xc, axis=-1, keepdims=True)
    vn = (xc * lax.rsqrt(var + LN_EPS) * lng_ref[...] + lnb_ref[...]).astype(BF16)
    row = lax.broadcasted_iota(jnp.int32, (GMLP_BLOCK, GMLP_BLOCK), 0) // STREAM_CHUNK
    col = lax.broadcasted_iota(jnp.int32, (GMLP_BLOCK, GMLP_BLOCK), 1) // STREAM_CHUNK
    causal = col <= row
    for h in range(heads):
        wm = jnp.where(causal, ws_ref[h], 0.0).astype(BF16)
        bias = bs_ref[h]
        cs = slice(h * V7X_LANES, (h + 1) * V7X_LANES)
        for n in range(nblk):
            rs = slice(n * GMLP_BLOCK, (n + 1) * GMLP_BLOCK)
            mixed = _mm(wm, vn[rs, cs]) + bias
            o_ref[rs, cs] = (jax.nn.gelu(u_ref[rs, cs]) * mixed).astype(o_ref.dtype)


def gmlp_gating(p_main, ln_g, ln_b, ws, bs, rows):
    n = p_main.shape[0]
    heads = ws.shape[0]
    gw = heads * V7X_LANES
    assert ws.shape[1:] == (GMLP_BLOCK, GMLP_BLOCK) and rows % GMLP_BLOCK == 0
    kern = functools.partial(_gmlp_kernel, heads=heads, nblk=rows // GMLP_BLOCK)
    return pl.pallas_call(
        kern,
        grid=(n // rows,),
        in_specs=[pl.BlockSpec((rows, gw), lambda i: (i, 0)),
                  pl.BlockSpec((rows, gw), lambda i: (i, 1)),
                  pl.BlockSpec((1, gw), lambda i: (0, 0)),
                  pl.BlockSpec((1, gw), lambda i: (0, 0)),
                  pl.BlockSpec((heads, GMLP_BLOCK, GMLP_BLOCK), lambda i: (0, 0, 0)),
                  pl.BlockSpec((heads, GMLP_BLOCK, 1), lambda i: (0, 0, 0))],
        out_specs=pl.BlockSpec((rows, gw), lambda i: (i, 0)),
        out_shape=jax.ShapeDtypeStruct((n, gw), BF16),
        compiler_params=_cparams("parallel"),
        name="gmlp_gating",
    )(p_main, p_main, ln_g.reshape(1, gw), ln_b.reshape(1, gw), ws, bs[:, :, None])


def _rwkv_prep_kernel(r_ref, k_ref, v_ref, t_ref, rh_ref, kh_ref, vh_ref, th_ref,
                      mur_ref, muk_ref, muv_ref, mut_ref, w0_ref, wup_ref, a0_ref, aup_ref, gup_ref,
                      kk_ref, ka_ref, rk_ref,
                      ro_ref, ko_ref, vo_ref, lw_ref, kko_ref, bo_ref, g_ref, bv_ref, *, blocks_per_seq):
    first = (pl.program_id(0) % blocks_per_seq) == 0

    def shifted(ref, halo_ref, mu_ref):
        p = ref[...]
        halo = jnp.where(first, 0.0, halo_ref[7:8, :])
        rows = lax.broadcasted_iota(jnp.int32, p.shape, 0)
        prev = jnp.where(rows == 0, halo, pltpu.roll(p, 1, 0))
        return p + (prev - p) * mu_ref[...]

    r = shifted(r_ref, rh_ref, mur_ref)
    k = shifted(k_ref, kh_ref, muk_ref)
    v = shifted(v_ref, vh_ref, muv_ref)
    tail = shifted(t_ref, th_ref, mut_ref)
    lora_in = tail[:, :V7X_LANES]
    xg = tail[:, V7X_LANES:]

    zw = w0_ref[...] + _mm(jnp.tanh(lora_in).astype(BF16), wup_ref[...])
    w_log = -jax.nn.softplus(-zw) - 0.5
    lw_ref[...] = -jnp.exp(w_log)
    a = jax.nn.sigmoid(a0_ref[...] + _mm(lora_in.astype(BF16), aup_ref[...]))
    g_ref[...] = _mm(jax.nn.sigmoid(xg).astype(BF16), gup_ref[...])

    ones_bd = _head_ones()
    kk = k * kk_ref[...]
    kk = kk / jnp.maximum(jnp.sqrt(_head_sum(kk * kk, ones_bd)), L2_EPS)
    k2 = k * (1.0 + (a - 1.0) * ka_ref[...])
    bonus = _head_sum(r * k2 * rk_ref[...], ones_bd)
    ro_ref[...] = r
    ko_ref[...] = k2
    vo_ref[...] = v
    kko_ref[...] = kk
    bo_ref[...] = kk * a
    bv_ref[...] = bonus * v


def rwkv_prep(p_main, p_tail, mu, w0, w_up, a0, a_up, g_up, k_k, k_a, r_k, tm, seq, first_col_block):
    n = p_main.shape[0]
    rw = w0.shape[0]
    dl, il = w_up.shape[0], a_up.shape[0]
    assert dl + il == V7X_LANES and seq % tm == 0 and tm % 8 == 0
    tw = p_tail.shape[1]
    cb = first_col_block
    row = lambda a: a.reshape(1, -1)
    mur, muk, muv, mut = mu[:rw], mu[rw:2 * rw], mu[2 * rw:3 * rw], mu[3 * rw:]
    wup_pad = jnp.concatenate([w_up, jnp.zeros((il, rw), w_up.dtype)], axis=0).astype(BF16)
    aup_pad = jnp.concatenate([jnp.zeros((dl, rw), a_up.dtype), a_up], axis=0).astype(BF16)
    blk = lambda c: pl.BlockSpec((tm, rw), lambda i: (i, c))
    halo = lambda c: pl.BlockSpec((8, rw), lambda i: (jnp.maximum(i * (tm // 8) - 1, 0), c))
    vec = lambda w: pl.BlockSpec((1, w), lambda i: (0, 0))
    full = lambda a: pl.BlockSpec(a.shape, lambda i: (0, 0))
    gup = g_up.astype(BF16)
    out_blk = pl.BlockSpec((tm, rw), lambda i: (i, 0))
    outs = pl.pallas_call(
        functools.partial(_rwkv_prep_kernel, blocks_per_seq=seq // tm),
        grid=(n // tm,),
        in_specs=[blk(cb), blk(cb + 1), blk(cb + 2), pl.BlockSpec((tm, tw), lambda i: (i, 0)),
                  halo(cb), halo(cb + 1), halo(cb + 2),
                  pl.BlockSpec((8, tw), lambda i: (jnp.maximum(i * (tm // 8) - 1, 0), 0)),
                  vec(rw), vec(rw), vec(rw), vec(tw), vec(rw), full(wup_pad), vec(rw), full(aup_pad), full(gup),
                  vec(rw), vec(rw), vec(rw)],
        out_specs=[out_blk] * 8,
        out_shape=[jax.ShapeDtypeStruct((n, rw), F32)] * 8,
        compiler_params=_cparams("parallel"),
        name="rwkv_prep",
    )(p_main, p_main, p_main, p_tail, p_main, p_main, p_main, p_tail,
      row(mur), row(muk), row(muv), row(mut), row(w0), wup_pad, row(a0), aup_pad, gup,
      row(k_k), row(k_a), row(r_k))
    return outs


def _stack_mask(x, lo):
    return jnp.concatenate([jnp.where(lo, x, 0.0), jnp.where(lo, 0.0, x)], axis=0)


def _scan_kernel(r_ref, k_ref, v_ref, lw_ref, kk_ref, b_ref, y_ref, state_ref, *, npairs, nchunks):
    C = SCAN_CHUNK
    L = V7X_LANES
    tc = nchunks * C

    @pl.when(pl.program_id(2) == 0)
    def _():
        state_ref[...] = jnp.zeros_like(state_ref)

    ti = lax.broadcasted_iota(jnp.int32, (tc, tc), 0)
    si = lax.broadcasted_iota(jnp.int32, (tc, tc), 1)
    tri = ((ti >= si) & (ti // C == si // C)).astype(F32)
    lw = lw_ref[...]
    cum = _mm(tri, lw, HIGHEST)

    lane = lax.broadcasted_iota(jnp.int32, (C, L), 1)
    rowi = lax.broadcasted_iota(jnp.int32, (C, L), 0)
    lo = lane < RWKV_HEAD_DIM
    scol = lane % RWKV_HEAD_DIM
    strict = rowi > scol
    incl = rowi >= scol
    eye_p = (rowi == scol).astype(F32)
    r2 = lax.broadcasted_iota(jnp.int32, (L, L), 0)
    c2 = lax.broadcasted_iota(jnp.int32, (L, L), 1)
    bd_mask = (r2 // RWKV_HEAD_DIM) == (c2 // RWKV_HEAD_DIM)
    eye2 = r2 == c2

    for p in range(npairs):
        cs = slice(p * L, (p + 1) * L)
        S = state_ref[p]
        for c in range(nchunks):
            rs = slice(c * C, (c + 1) * C)
            cm = cum[rs, cs]
            lwc = lw[rs, cs]
            r = r_ref[rs, cs]
            k = k_ref[rs, cs]
            v = v_ref[rs, cs]
            kk = kk_ref[rs, cs]
            beta = b_ref[rs, cs]
            cl = jnp.broadcast_to(cm[C - 1:C, :], (C, L))
            ginv = jnp.exp(-cm)
            a_t = -kk * jnp.exp(cm - lwc)
            b_t = beta * ginv
            k_t = k * ginv
            r_t = r * jnp.exp(cm)
            dec_end = jnp.exp(cl - cm)
            b_h = beta * dec_end
            k_h = k * dec_end

            lhs = jnp.concatenate([a_t, r_t], axis=0)
            rhs = jnp.concatenate([_stack_mask(b_t, lo), _stack_mask(k_t, lo)], axis=0)
            sc = _nt(lhs, rhs, HIGHEST)
            l_ab = jnp.where(strict, sc[:C, :L], 0.0)
            l_ak = jnp.where(strict, sc[:C, L:], 0.0)
            p_rb = jnp.where(incl, sc[C:, :L], 0.0)
            p_rk = jnp.where(incl, sc[C:, L:], 0.0)

            nfac = (C - 1).bit_length()
            t_inv = eye_p + l_ab
            pw = _mm(l_ab, _stack_mask(l_ab, lo), HIGHEST)
            for i in range(1, nfac):
                if i < nfac - 1:
                    both = _mm(pw, jnp.concatenate([_stack_mask(pw, lo), _stack_mask(t_inv, lo)], axis=1),
                               HIGHEST)
                    t_inv = t_inv + both[:, L:]
                    pw = both[:, :L]
                else:
                    t_inv = t_inv + _mm(pw, _stack_mask(t_inv, lo), HIGHEST)

            x1 = _mm(l_ak, _stack_mask(v, lo), HIGHEST)
            w_t = _mm(t_inv, _stack_mask(a_t, lo), HIGHEST)
            v_t = _mm(t_inv, _stack_mask(x1, lo), HIGHEST)
            q = r_t + _mm(p_rb, _stack_mask(w_t, lo), HIGHEST)
            y0 = _mm(p_rb, _stack_mask(v_t, lo), HIGHEST) + _mm(p_rk, _stack_mask(v, lo), HIGHEST)
            m_bd = jnp.where(bd_mask, _tn(b_h, w_t, HIGHEST), 0.0)
            n_bd = jnp.where(bd_mask, _tn(b_h, v_t, HIGHEST) + _tn(k_h, v, HIGHEST), 0.0)
            gcol = jnp.sum(jnp.where(eye2, jnp.exp(jnp.broadcast_to(cm[C - 1:C, :], (L, L))), 0.0),
                           axis=1, keepdims=True)

            y_ref[rs, cs] = _mm(q, S, HIGHEST) + y0
            S = gcol * S + _mm(m_bd, S, HIGHEST) + n_bd
        state_ref[p] = S


def rwkv_scan(r, k, v, lw, kk, beta, batch, seq, tc, npairs):
    n, rw = r.shape
    L = V7X_LANES
    assert seq % tc == 0 and tc % SCAN_CHUNK == 0 and rw % (npairs * L) == 0
    nt = seq // tc
    blk = pl.BlockSpec((tc, npairs * L), lambda b, g, c: (b * nt + c, g))
    return pl.pallas_call(
        functools.partial(_scan_kernel, npairs=npairs, nchunks=tc // SCAN_CHUNK),
        grid=(batch, rw // (npairs * L), nt),
        in_specs=[blk] * 6,
        out_specs=blk,
        out_shape=jax.ShapeDtypeStruct((n, rw), F32),
        scratch_shapes=[pltpu.VMEM((npairs, L, L), F32)],
        compiler_params=_cparams("parallel", "parallel", "arbitrary"),
        name="rwkv_scan",
    )(r, k, v, lw, kk, beta)


def _rwkv_post_kernel(y_ref, bv_ref, g_ref, lg_ref, lb_ref, o_ref):
    ones_bd = _head_ones()
    y = y_ref[...]
    inv_n = 1.0 / RWKV_HEAD_DIM
    yc = y - _head_sum(y, ones_bd) * inv_n
    var = _head_sum(yc * yc, ones_bd) * inv_n
    yn = yc * lax.rsqrt(var + RWKV_HEAD_DIM * GN_EPS_PER_CH) * lg_ref[...] + lb_ref[...]
    o_ref[...] = ((yn + bv_ref[...]) * g_ref[...]).astype(o_ref.dtype)


def rwkv_post(y, bv, g, lnx_g, lnx_b, tm):
    n, rw = y.shape
    blk = pl.BlockSpec((tm, rw), lambda i: (i, 0))
    vec = pl.BlockSpec((1, rw), lambda i: (0, 0))
    return pl.pallas_call(
        _rwkv_post_kernel,
        grid=(n // tm,),
        in_specs=[blk, blk, blk, vec, vec],
        out_specs=blk,
        out_shape=jax.ShapeDtypeStruct((n, rw), BF16),
        compiler_params=_cparams("parallel"),
        name="rwkv_post",
    )(y, bv, g, lnx_g.reshape(1, rw), lnx_b.reshape(1, rw))


def _post_mix_kernel(mix_ref, x_ref, gpost_ref, gpre_ref, x1_ref, xn_ref):
    m = mix_ref[...]
    x1 = x_ref[...] + m * lax.rsqrt(jnp.mean(m * m, axis=-1, keepdims=True) + RMS_EPS) * gpost_ref[...]
    x1_ref[...] = x1
    xn_ref[...] = (x1 * lax.rsqrt(jnp.mean(x1 * x1, axis=-1, keepdims=True) + RMS_EPS)
                   * gpre_ref[...]).astype(xn_ref.dtype)


def post_mix(mix, x, g_post, g_pre, tm):
    n, d = x.shape
    blk = pl.BlockSpec((tm, d), lambda i: (i, 0))
    vec = pl.BlockSpec((1, d), lambda i: (0, 0))
    return pl.pallas_call(
        _post_mix_kernel,
        grid=(n // tm,),
        in_specs=[blk, blk, vec, vec],
        out_specs=[blk, blk],
        out_shape=[jax.ShapeDtypeStruct((n, d), F32), jax.ShapeDtypeStruct((n, d), BF16)],
        compiler_params=_cparams("parallel"),
        name="post_mix",
    )(mix, x, g_post.reshape(1, d), g_pre.reshape(1, d))


def _post_ffn_kernel(f_ref, x_ref, g_ref, o_ref):
    f = f_ref[...]
    o_ref[...] = x_ref[...] + f * lax.rsqrt(jnp.mean(f * f, axis=-1, keepdims=True) + RMS_EPS) * g_ref[...]


def post_ffn(f, x1, g, tm):
    n, d = x1.shape
    blk = pl.BlockSpec((tm, d), lambda i: (i, 0))
    return pl.pallas_call(
        _post_ffn_kernel,
        grid=(n // tm,),
        in_specs=[blk, blk, pl.BlockSpec((1, d), lambda i: (0, 0))],
        out_specs=blk,
        out_shape=jax.ShapeDtypeStruct((n, d), F32),
        compiler_params=_cparams("parallel"),
        name="post_ffn",
    )(f, x1, g.reshape(1, d))


def _ffn_kernel(xn_ref, w1_ref, w2_ref, o_ref):
    h = jnp.maximum(_mm(xn_ref[...], w1_ref[...]), 0.0)
    part = _mm((h * h).astype(BF16), w2_ref[...])

    @pl.when(pl.program_id(1) == 0)
    def _():
        o_ref[...] = part

    @pl.when(pl.program_id(1) != 0)
    def _():
        o_ref[...] += part


def ffn(xn, w1, w2, tm, tf):
    n, d = xn.shape
    f = w1.shape[1]
    return pl.pallas_call(
        _ffn_kernel,
        grid=(n // tm, f // tf),
        in_specs=[pl.BlockSpec((tm, d), lambda i, j: (i, 0)),
                  pl.BlockSpec((d, tf), lambda i, j: (0, j)),
                  pl.BlockSpec((tf, d), lambda i, j: (j, 0))],
        out_specs=pl.BlockSpec((tm, d), lambda i, j: (i, 0)),
        out_shape=jax.ShapeDtypeStruct((n, d), F32),
        compiler_params=_cparams("parallel", "arbitrary"),
        name="ffn",
    )(xn, w1, w2)


def _tile(n, want):
    if n <= want:
        return n
    t = want - want % V7X_LANES
    while n % t:
        t -= V7X_LANES
    assert t > 0, (n, want)
    return t


def kernel(x, pre_mix_g, w_in, tshift_mu, gmlp_ln_g, gmlp_ln_b, gmlp_ws, gmlp_bs, decay_w0, decay_up, iclr_a0, iclr_up, gate_up, k_k, k_a, r_k, lnx_g, lnx_b, w_out, post_mix_g, pre_ffn_g, w_ff1, w_ff2, post_ffn_g):
    batch, seq, d = x.shape
    depth = w_in.shape[0]
    n = batch * seq
    gw = gmlp_ws.shape[1] * V7X_LANES
    rw = decay_w0.shape[1]
    main_w = 2 * gw + 3 * rw
    assert gw == rw, "column-block indexing below assumes equal head-group widths"
    xf = x.reshape(n, d)
    for l in range(depth):
        w_in_main = w_in[l, :, :main_w].astype(BF16)
        w_in_tail = w_in[l, :, main_w:].astype(BF16)
        h = rms_norm_bf16(xf, pre_mix_g[l], _tile(n, 512))
        p_main = matmul(h, w_in_main, _tile(n, 1024), _tile(main_w, 1024), "w_in_main")
        p_tail = matmul(h, w_in_tail, _tile(n, 1024), w_in_tail.shape[1], "w_in_tail")

        y_a = gmlp_gating(p_main, gmlp_ln_g[l], gmlp_ln_b[l], gmlp_ws[l], gmlp_bs[l], _tile(seq, 512))

        r, k2, v, lw, kk, beta, g, bv = rwkv_prep(
            p_main, p_tail, tshift_mu[l], decay_w0[l], decay_up[l], iclr_a0[l], iclr_up[l], gate_up[l],
            k_k[l], k_a[l], r_k[l].reshape(-1), _tile(seq, 128), seq, first_col_block=2)
        y = rwkv_scan(r, k2, v, lw, kk, beta, batch, seq, _tile(seq, 256), npairs=2)
        y_b = rwkv_post(y, bv, g, lnx_g[l], lnx_b[l], _tile(n, 512))

        mix = matmul_concat2(y_a, y_b, w_out[l].astype(BF16), _tile(n, 1024), _tile(d, 1024), "w_out")
        x1, xn = post_mix(mix, xf, post_mix_g[l], pre_ffn_g[l], _tile(n, 256))
        f = ffn(xn, w_ff1[l].astype(BF16), w_ff2[l].astype(BF16), _tile(n, 512), _tile(w_ff1.shape[2], 512))
        xf = post_ffn(f, x1, post_ffn_g[l], _tile(n, 256))
    return xf.reshape(batch, seq, d)
```

```python
import functools

import jax
import jax.numpy as jnp
from jax import lax
from jax.experimental import pallas as pl
from jax.experimental.pallas import tpu as pltpu

F32 = jnp.float32
BF16 = jnp.bfloat16
HIGHEST = lax.Precision.HIGHEST

RMS_EPS = 1e-6
LN_EPS = 1e-5
L2_EPS = 1e-12
GN_EPS_PER_CH = 1e-5

V7X_LANES = 128
RWKV_HEAD_DIM = 64
GMLP_BLOCK = 128
STREAM_CHUNK = 64
SCAN_CHUNK = 64
VMEM_LIMIT = 56 * 1024 * 1024


def _cparams(*sem):
    return pltpu.CompilerParams(dimension_semantics=sem, vmem_limit_bytes=VMEM_LIMIT)


def _nt(a, b, precision=None):
    return lax.dot_general(a, b, (((1,), (1,)), ((), ())), precision=precision,
                           preferred_element_type=F32)


def _tn(a, b, precision=None):
    return lax.dot_general(a, b, (((0,), (0,)), ((), ())), precision=precision,
                           preferred_element_type=F32)


def _mm(a, b, precision=None):
    return jnp.dot(a, b, precision=precision, preferred_element_type=F32)


def _head_ones(n=V7X_LANES, head=RWKV_HEAD_DIM):
    r = lax.broadcasted_iota(jnp.int32, (n, n), 0) // head
    c = lax.broadcasted_iota(jnp.int32, (n, n), 1) // head
    return (r == c).astype(F32)


def _head_sum(x, ones_bd):
    w = x.shape[1]
    parts = [_mm(x[:, j:j + V7X_LANES], ones_bd, HIGHEST) for j in range(0, w, V7X_LANES)]
    return parts[0] if len(parts) == 1 else jnp.concatenate(parts, axis=1)


def _rms_kernel(x_ref, g_ref, o_ref):
    x = x_ref[...]
    ms = jnp.mean(x * x, axis=-1, keepdims=True)
    o_ref[...] = (x * lax.rsqrt(ms + RMS_EPS) * g_ref[...]).astype(o_ref.dtype)


def rms_norm_bf16(x, g, tm):
    n, d = x.shape
    return pl.pallas_call(
        _rms_kernel,
        grid=(n // tm,),
        in_specs=[pl.BlockSpec((tm, d), lambda i: (i, 0)), pl.BlockSpec((1, d), lambda i: (0, 0))],
        out_specs=pl.BlockSpec((tm, d), lambda i: (i, 0)),
        out_shape=jax.ShapeDtypeStruct((n, d), BF16),
        compiler_params=_cparams("parallel"),
        name="rms_norm",
    )(x, g.reshape(1, d))


def _mm_kernel(a_ref, w_ref, o_ref):
    o_ref[...] = _mm(a_ref[...], w_ref[...]).astype(o_ref.dtype)


def matmul(a, w, tm, tn, name):
    m, k = a.shape
    n = w.shape[1]
    return pl.pallas_call(
        _mm_kernel,
        grid=(m // tm, n // tn),
        in_specs=[pl.BlockSpec((tm, k), lambda i, j: (i, 0)), pl.BlockSpec((k, tn), lambda i, j: (0, j))],
        out_specs=pl.BlockSpec((tm, tn), lambda i, j: (i, j)),
        out_shape=jax.ShapeDtypeStruct((m, n), F32),
        compiler_params=_cparams("parallel", "arbitrary"),
        name=name,
    )(a, w)


def _mm2_kernel(a1_ref, a2_ref, w1_ref, w2_ref, o_ref):
    o_ref[...] = _mm(a1_ref[...], w1_ref[...]) + _mm(a2_ref[...], w2_ref[...])


def matmul_concat2(a1, a2, w, tm, tn, name):
    m, k1 = a1.shape
    k2 = a2.shape[1]
    n = w.shape[1]
    nb1 = 1
    assert k1 == k2
    return pl.pallas_call(
        _mm2_kernel,
        grid=(m // tm, n // tn),
        in_specs=[pl.BlockSpec((tm, k1), lambda i, j: (i, 0)),
                  pl.BlockSpec((tm, k2), lambda i, j: (i, 0)),
                  pl.BlockSpec((k1, tn), lambda i, j: (0, j)),
                  pl.BlockSpec((k2, tn), lambda i, j: (nb1, j))],
        out_specs=pl.BlockSpec((tm, tn), lambda i, j: (i, j)),
        out_shape=jax.ShapeDtypeStruct((m, n), F32),
        compiler_params=_cparams("parallel", "arbitrary"),
        name=name,
    )(a1, a2, w, w)


def _gmlp_kernel(u_ref, v_ref, lng_ref, lnb_ref, ws_ref, bs_ref, o_ref, *, heads, nblk):
    zv = jax.nn.gelu(v_ref[...])
    mean = jnp.mean(zv, axis=-1, keepdims=True)
    xc = zv - mean
    var = jnp.mean(xc * xc, axis=-1, keepdims=True)
    vn = (xc * lax.rsqrt(var + LN_EPS) * lng_ref[...] + lnb_ref[...]).astype(BF16)
    row = lax.broadcasted_iota(jnp.int32, (GMLP_BLOCK, GMLP_BLOCK), 0) // STREAM_CHUNK
    col = lax.broadcasted_iota(jnp.int32, (GMLP_BLOCK, GMLP_BLOCK), 1) // STREAM_CHUNK
    causal = col <= row
    for h in range(heads):
        wm = jnp.where(causal, ws_ref[h], 0.0).astype(BF16)
        bias = bs_ref[h]
        cs = slice(h * V7X_LANES, (h + 1) * V7X_LANES)
        for n in range(nblk):
            rs = slice(n * GMLP_BLOCK, (n + 1) * GMLP_BLOCK)
            mixed = _mm(wm, vn[rs, cs]) + bias
            o_ref[rs, cs] = (jax.nn.gelu(u_ref[rs, cs]) * mixed).astype(o_ref.dtype)


def gmlp_gating(p_main, ln_g, ln_b, ws, bs, rows):
    n = p_main.shape[0]
    heads = ws.shape[0]
    gw = heads * V7X_LANES
    assert ws.shape[1:] == (GMLP_BLOCK, GMLP_BLOCK) and rows % GMLP_BLOCK == 0
    kern = functools.partial(_gmlp_kernel, heads=heads, nblk=rows // GMLP_BLOCK)
    return pl.pallas_call(
        kern,
        grid=(n // rows,),
        in_specs=[pl.BlockSpec((rows, gw), lambda i: (i, 0)),
                  pl.BlockSpec((rows, gw), lambda i: (i, 1)),
                  pl.BlockSpec((1, gw), lambda i: (0, 0)),
                  pl.BlockSpec((1, gw), lambda i: (0, 0)),
                  pl.BlockSpec((heads, GMLP_BLOCK, GMLP_BLOCK), lambda i: (0, 0, 0)),
                  pl.BlockSpec((heads, GMLP_BLOCK, 1), lambda i: (0, 0, 0))],
        out_specs=pl.BlockSpec((rows, gw), lambda i: (i, 0)),
        out_shape=jax.ShapeDtypeStruct((n, gw), BF16),
        compiler_params=_cparams("parallel"),
        name="gmlp_gating",
    )(p_main, p_main, ln_g.reshape(1, gw), ln_b.reshape(1, gw), ws, bs[:, :, None])


def _rwkv_prep_kernel(r_ref, k_ref, v_ref, t_ref, rh_ref, kh_ref, vh_ref, th_ref,
                      mur_ref, muk_ref, muv_ref, mut_ref, w0_ref, wup_ref, a0_ref, aup_ref, gup_ref,
                      kk_ref, ka_ref, rk_ref,
                      ro_ref, ko_ref, vo_ref, lw_ref, kko_ref, bo_ref, g_ref, bv_ref, *, blocks_per_seq):
    first = (pl.program_id(0) % blocks_per_seq) == 0

    def shifted(ref, halo_ref, mu_ref):
        p = ref[...]
        halo = jnp.where(first, 0.0, halo_ref[7:8, :])
        rows = lax.broadcasted_iota(jnp.int32, p.shape, 0)
        prev = jnp.where(rows == 0, halo, pltpu.roll(p, 1, 0))
        return p + (prev - p) * mu_ref[...]

    r = shifted(r_ref, rh_ref, mur_ref)
    k = shifted(k_ref, kh_ref, muk_ref)
    v = shifted(v_ref, vh_ref, muv_ref)
    tail = shifted(t_ref, th_ref, mut_ref)
    lora_in = tail[:, :V7X_LANES]
    xg = tail[:, V7X_LANES:]

    zw = w0_ref[...] + _mm(jnp.tanh(lora_in).astype(BF16), wup_ref[...])
    w_log = -jax.nn.softplus(-zw) - 0.5
    lw_ref[...] = -jnp.exp(w_log)
    a = jax.nn.sigmoid(a0_ref[...] + _mm(lora_in.astype(BF16), aup_ref[...]))
    g_ref[...] = _mm(jax.nn.sigmoid(xg).astype(BF16), gup_ref[...])

    ones_bd = _head_ones()
    kk = k * kk_ref[...]
    kk = kk / jnp.maximum(jnp.sqrt(_head_sum(kk * kk, ones_bd)), L2_EPS)
    k2 = k * (1.0 + (a - 1.0) * ka_ref[...])
    bonus = _head_sum(r * k2 * rk_ref[...], ones_bd)
    ro_ref[...] = r
    ko_ref[...] = k2
    vo_ref[...] = v
    kko_ref[...] = kk
    bo_ref[...] = kk * a
    bv_ref[...] = bonus * v


def rwkv_prep(p_main, p_tail, mu, w0, w_up, a0, a_up, g_up, k_k, k_a, r_k, tm, seq, first_col_block):
    n = p_main.shape[0]
    rw = w0.shape[0]
    dl, il = w_up.shape[0], a_up.shape[0]
    assert dl + il == V7X_LANES and seq % tm == 0 and tm % 8 == 0
    tw = p_tail.shape[1]
    cb = first_col_block
    row = lambda a: a.reshape(1, -1)
    mur, muk, muv, mut = mu[:rw], mu[rw:2 * rw], mu[2 * rw:3 * rw], mu[3 * rw:]
    wup_pad = jnp.concatenate([w_up, jnp.zeros((il, rw), w_up.dtype)], axis=0).astype(BF16)
    aup_pad = jnp.concatenate([jnp.zeros((dl, rw), a_up.dtype), a_up], axis=0).astype(BF16)
    blk = lambda c: pl.BlockSpec((tm, rw), lambda i: (i, c))
    halo = lambda c: pl.BlockSpec((8, rw), lambda i: (jnp.maximum(i * (tm // 8) - 1, 0), c))
    vec = lambda w: pl.BlockSpec((1, w), lambda i: (0, 0))
    full = lambda a: pl.BlockSpec(a.shape, lambda i: (0, 0))
    gup = g_up.astype(BF16)
    out_blk = pl.BlockSpec((tm, rw), lambda i: (i, 0))
    outs = pl.pallas_call(
        functools.partial(_rwkv_prep_kernel, blocks_per_seq=seq // tm),
        grid=(n // tm,),
        in_specs=[blk(cb), blk(cb + 1), blk(cb + 2), pl.BlockSpec((tm, tw), lambda i: (i, 0)),
                  halo(cb), halo(cb + 1), halo(cb + 2),
                  pl.BlockSpec((8, tw), lambda i: (jnp.maximum(i * (tm // 8) - 1, 0), 0)),
                  vec(rw), vec(rw), vec(rw), vec(tw), vec(rw), full(wup_pad), vec(rw), full(aup_pad), full(gup),
                  vec(rw), vec(rw), vec(rw)],
        out_specs=[out_blk] * 8,
        out_shape=[jax.ShapeDtypeStruct((n, rw), F32)] * 8,
        compiler_params=_cparams("parallel"),
        name="rwkv_prep",
    )(p_main, p_main, p_main, p_tail, p_main, p_main, p_main, p_tail,
      row(mur), row(muk), row(muv), row(mut), row(w0), wup_pad, row(a0), aup_pad, gup,
      row(k_k), row(k_a), row(r_k))
    return outs


def _scan_kernel(r_ref, k_ref, v_ref, lw_ref, kk_ref, b_ref, y_ref, state_ref, *, heads, nchunks):
    C = SCAN_CHUNK
    N = RWKV_HEAD_DIM
    L = heads * N
    tc = nchunks * C

    @pl.when(pl.program_id(2) == 0)
    def _():
        state_ref[...] = jnp.zeros_like(state_ref)

    ti = lax.broadcasted_iota(jnp.int32, (tc, tc), 0)
    si = lax.broadcasted_iota(jnp.int32, (tc, tc), 1)
    tri = ((ti >= si) & (ti // C == si // C)).astype(F32)
    lw_all = lw_ref[...]
    cum_all = _mm(tri, lw_all, HIGHEST)

    lane = lax.broadcasted_iota(jnp.int32, (C, L), 1)
    rowi = lax.broadcasted_iota(jnp.int32, (C, L), 0)
    lane_h = lane // N
    scol = lane % N
    strict = rowi > scol
    incl = rowi >= scol
    eye_p = (rowi == scol).astype(F32)
    r2 = lax.broadcasted_iota(jnp.int32, (L, L), 0)
    c2 = lax.broadcasted_iota(jnp.int32, (L, L), 1)
    bd_mask = (r2 // N) == (c2 // N)
    eye2 = r2 == c2
    head_sel = [lane_h == h for h in range(heads)]

    def sm(x):
        xb = x.astype(BF16)
        zero = jnp.zeros_like(xb)
        return jnp.concatenate([jnp.where(sel, xb, zero) for sel in head_sel], axis=0)

    def mmb(a, b):
        return _mm(a.astype(BF16), b.astype(BF16))

    chunks = range(nchunks)

    def each(fn, *lists):
        return [fn(*xs) for xs in zip(*lists)]

    rows = [slice(c * C, (c + 1) * C) for c in chunks]
    cm = [cum_all[rs] for rs in rows]
    r = [r_ref[rs, :] for rs in rows]
    k = [k_ref[rs, :] for rs in rows]
    v = [v_ref[rs, :] for rs in rows]
    kk = [kk_ref[rs, :] for rs in rows]
    beta = [b_ref[rs, :] for rs in rows]
    ginv = each(lambda m: jnp.exp(-m), cm)
    a_t = each(lambda kk_, m, rs: -kk_ * jnp.exp(m - lw_all[rs]), kk, cm, rows)
    b_t = each(jnp.multiply, beta, ginv)
    k_t = each(jnp.multiply, k, ginv)
    r_t = each(lambda r_, m: r_ * jnp.exp(m), r, cm)
    dec_end = each(lambda m: jnp.exp(jnp.broadcast_to(m[C - 1:C, :], (C, L)) - m), cm)
    b_h = each(jnp.multiply, beta, dec_end)
    k_h = each(jnp.multiply, k, dec_end)

    sc = each(lambda a_, r_, b_, k_: _nt(jnp.concatenate([a_, r_], axis=0).astype(BF16),
                                         jnp.concatenate([sm(b_), sm(k_)], axis=0)),
              a_t, r_t, b_t, k_t)
    ab = [x[:C, :L] for x in sc]
    l_ak = [jnp.where(strict, x[:C, L:], 0.0) for x in sc]
    p_rb = [jnp.where(incl, x[C:, :L], 0.0) for x in sc]
    p_rk = [jnp.where(incl, x[C:, L:], 0.0) for x in sc]

    base = 8
    d = [jnp.where(strict & (rowi // base == scol // base), x, 0.0) for x in ab]
    t_inv = [eye_p + x for x in d]
    p2 = each(lambda x: mmb(x, sm(x)), d)
    both = each(lambda p, t: mmb(p, jnp.concatenate([sm(p), sm(t)], axis=1)), p2, t_inv)
    t_inv = each(lambda t, bo: t + bo[:, L:], t_inv, both)
    t_inv = each(lambda t, bo: t + mmb(bo[:, :L], sm(t)), t_inv, both)
    b = base
    while b < C:
        blk_off = (rowi // (2 * b) == scol // (2 * b)) & (rowi // b > scol // b)
        ot = each(lambda x, t: mmb(jnp.where(blk_off, x, 0.0), sm(t)), ab, t_inv)
        t_inv = each(lambda t, o: t + mmb(t, sm(o)), t_inv, ot)
        b *= 2

    sm_v = each(sm, v)
    x1 = each(mmb, l_ak, sm_v)
    wv = each(lambda t, a_, x_: mmb(t, jnp.concatenate([sm(a_), sm(x_)], axis=1)), t_inv, a_t, x1)
    w_t = [x[:, :L] for x in wv]
    v_t = [x[:, L:] for x in wv]
    q = each(lambda r_, p, w: (r_ + mmb(p, sm(w))).astype(BF16), r_t, p_rb, w_t)
    y0 = each(lambda pb, pk, vt, sv: mmb(jnp.concatenate([pb, pk], axis=1), jnp.concatenate([sm(vt), sv], axis=0)),
              p_rb, p_rk, v_t, sm_v)
    b_ht = [x.T for x in b_h]
    k_ht = [x.T for x in k_h]
    m_bd = each(lambda bt, w: jnp.where(bd_mask, mmb(bt, w), 0.0).astype(BF16), b_ht, w_t)
    n_bd = each(lambda bt, kt, vt, v_: jnp.where(bd_mask, mmb(jnp.concatenate([bt, kt], axis=1),
                                                               jnp.concatenate([vt, v_], axis=0)), 0.0),
                b_ht, k_ht, v_t, v)
    gcol = each(lambda m: jnp.sum(jnp.where(eye2, jnp.exp(jnp.broadcast_to(m[C - 1:C, :], (L, L))), 0.0),
                                  axis=1, keepdims=True), cm)

    S = state_ref[...]
    for c in chunks:
        sb = S.astype(BF16)
        y_ref[rows[c], :] = _mm(q[c], sb) + y0[c]
        S = gcol[c] * S + _mm(m_bd[c], sb) + n_bd[c]
    state_ref[...] = S


def rwkv_scan(r, k, v, lw, kk, beta, batch, seq, tc, heads):
    n, rw = r.shape
    L = heads * RWKV_HEAD_DIM
    assert seq % tc == 0 and tc % SCAN_CHUNK == 0 and rw % L == 0 and L % V7X_LANES == 0
    nt = seq // tc
    blk = pl.BlockSpec((tc, L), lambda b, g, c: (b * nt + c, g))
    return pl.pallas_call(
        functools.partial(_scan_kernel, heads=heads, nchunks=tc // SCAN_CHUNK),
        grid=(batch, rw // L, nt),
        in_specs=[blk] * 6,
        out_specs=blk,
        out_shape=jax.ShapeDtypeStruct((n, rw), F32),
        scratch_shapes=[pltpu.VMEM((L, L), F32)],
        compiler_params=_cparams("parallel", "parallel", "arbitrary"),
        name="rwkv_scan",
    )(r, k, v, lw, kk, beta)


def _rwkv_post_kernel(y_ref, bv_ref, g_ref, lg_ref, lb_ref, o_ref):
    ones_bd = _head_ones()
    y = y_ref[...]
    inv_n = 1.0 / RWKV_HEAD_DIM
    yc = y - _head_sum(y, ones_bd) * inv_n
    var = _head_sum(yc * yc, ones_bd) * inv_n
    yn = yc * lax.rsqrt(var + RWKV_HEAD_DIM * GN_EPS_PER_CH) * lg_ref[...] + lb_ref[...]
    o_ref[...] = ((yn + bv_ref[...]) * g_ref[...]).astype(o_ref.dtype)


def rwkv_post(y, bv, g, lnx_g, lnx_b, tm):
    n, rw = y.shape
    blk = pl.BlockSpec((tm, rw), lambda i: (i, 0))
    vec = pl.BlockSpec((1, rw), lambda i: (0, 0))
    return pl.pallas_call(
        _rwkv_post_kernel,
        grid=(n // tm,),
        in_specs=[blk, blk, blk, vec, vec],
        out_specs=blk,
        out_shape=jax.ShapeDtypeStruct((n, rw), BF16),
        compiler_params=_cparams("parallel"),
        name="rwkv_post",
    )(y, bv, g, lnx_g.reshape(1, rw), lnx_b.reshape(1, rw))


def _post_mix_kernel(mix_ref, x_ref, gpost_ref, gpre_ref, x1_ref, xn_ref):
    m = mix_ref[...]
    x1 = x_ref[...] + m * lax.rsqrt(jnp.mean(m * m, axis=-1, keepdims=True) + RMS_EPS) * gpost_ref[...]
    x1_ref[...] = x1
    xn_ref[...] = (x1 * lax.rsqrt(jnp.mean(x1 * x1, axis=-1, keepdims=True) + RMS_EPS)
                   * gpre_ref[...]).astype(xn_ref.dtype)


def post_mix(mix, x, g_post, g_pre, tm):
    n, d = x.shape
    blk = pl.BlockSpec((tm, d), lambda i: (i, 0))
    vec = pl.BlockSpec((1, d), lambda i: (0, 0))
    return pl.pallas_call(
        _post_mix_kernel,
        grid=(n // tm,),
        in_specs=[blk, blk, vec, vec],
        out_specs=[blk, blk],
        out_shape=[jax.ShapeDtypeStruct((n, d), F32), jax.ShapeDtypeStruct((n, d), BF16)],
        compiler_params=_cparams("parallel"),
        name="post_mix",
    )(mix, x, g_post.reshape(1, d), g_pre.reshape(1, d))


def _post_ffn_kernel(f_ref, x_ref, g_ref, o_ref):
    f = f_ref[...]
    o_ref[...] = x_ref[...] + f * lax.rsqrt(jnp.mean(f * f, axis=-1, keepdims=True) + RMS_EPS) * g_ref[...]


def post_ffn(f, x1, g, tm):
    n, d = x1.shape
    blk = pl.BlockSpec((tm, d), lambda i: (i, 0))
    return pl.pallas_call(
        _post_ffn_kernel,
        grid=(n // tm,),
        in_specs=[blk, blk, pl.BlockSpec((1, d), lambda i: (0, 0))],
        out_specs=blk,
        out_shape=jax.ShapeDtypeStruct((n, d), F32),
        compiler_params=_cparams("parallel"),
        name="post_ffn",
    )(f, x1, g.reshape(1, d))


def _ffn_kernel(xn_ref, w1_ref, w2_ref, o_ref):
    h = jnp.maximum(_mm(xn_ref[...], w1_ref[...]), 0.0)
    part = _mm((h * h).astype(BF16), w2_ref[...])

    @pl.when(pl.program_id(1) == 0)
    def _():
        o_ref[...] = part

    @pl.when(pl.program_id(1) != 0)
    def _():
        o_ref[...] += part


def ffn(xn, w1, w2, tm, tf):
    n, d = xn.shape
    f = w1.shape[1]
    return pl.pallas_call(
        _ffn_kernel,
        grid=(n // tm, f // tf),
        in_specs=[pl.BlockSpec((tm, d), lambda i, j: (i, 0)),
                  pl.BlockSpec((d, tf), lambda i, j: (0, j)),
                  pl.BlockSpec((tf, d), lambda i, j: (j, 0))],
        out_specs=pl.BlockSpec((tm, d), lambda i, j: (i, 0)),
        out_shape=jax.ShapeDtypeStruct((n, d), F32),
        compiler_params=_cparams("parallel", "arbitrary"),
        name="ffn",
    )(xn, w1, w2)


def _tile(n, want):
    if n <= want:
        return n
    t = want - want % V7X_LANES
    while n % t:
        t -= V7X_LANES
    assert t > 0, (n, want)
    return t


def kernel(x, pre_mix_g, w_in, tshift_mu, gmlp_ln_g, gmlp_ln_b, gmlp_ws, gmlp_bs, decay_w0, decay_up, iclr_a0, iclr_up, gate_up, k_k, k_a, r_k, lnx_g, lnx_b, w_out, post_mix_g, pre_ffn_g, w_ff1, w_ff2, post_ffn_g):
    batch, seq, d = x.shape
    depth = w_in.shape[0]
    n = batch * seq
    gw = gmlp_ws.shape[1] * V7X_LANES
    rw = decay_w0.shape[1]
    main_w = 2 * gw + 3 * rw
    assert gw == rw, "column-block indexing below assumes equal head-group widths"
    xf = x.reshape(n, d)
    for l in range(depth):
        w_in_main = w_in[l, :, :main_w].astype(BF16)
        w_in_tail = w_in[l, :, main_w:].astype(BF16)
        h = rms_norm_bf16(xf, pre_mix_g[l], _tile(n, 512))
        p_main = matmul(h, w_in_main, _tile(n, 1024), _tile(main_w, 1024), "w_in_main")
        p_tail = matmul(h, w_in_tail, _tile(n, 1024), w_in_tail.shape[1], "w_in_tail")

        y_a = gmlp_gating(p_main, gmlp_ln_g[l], gmlp_ln_b[l], gmlp_ws[l], gmlp_bs[l], _tile(seq, 512))

        r, k2, v, lw, kk, beta, g, bv = rwkv_prep(
            p_main, p_tail, tshift_mu[l], decay_w0[l], decay_up[l], iclr_a0[l], iclr_up[l], gate_up[l],
            k_k[l], k_a[l], r_k[l].reshape(-1), _tile(seq, 128), seq, first_col_block=2)
        y = rwkv_scan(r, k2, v, lw, kk, beta, batch, seq, _tile(seq, 256), heads=4)
        y_b = rwkv_post(y, bv, g, lnx_g[l], lnx_b[l], _tile(n, 512))

        mix = matmul_concat2(y_a, y_b, w_out[l].astype(BF16), _tile(n, 1024), _tile(d, 1024), "w_out")
        x1, xn = post_mix(mix, xf, post_mix_g[l], pre_ffn_g[l], _tile(n, 256))
        f = ffn(xn, w_ff1[l].astype(BF16), w_ff2[l].astype(BF16), _tile(n, 512), _tile(w_ff1.shape[2], 512))
        xf = post_ffn(f, x1, post_ffn_g[l], _tile(n, 256))
    return xf.reshape(batch, seq, d)
```

```python
import functools

import jax
import jax.numpy as jnp
from jax import lax
from jax.experimental import pallas as pl
from jax.experimental.pallas import tpu as pltpu

F32 = jnp.float32
BF16 = jnp.bfloat16
HIGHEST = lax.Precision.HIGHEST

RMS_EPS = 1e-6
LN_EPS = 1e-5
L2_EPS = 1e-12
GN_EPS_PER_CH = 1e-5

V7X_LANES = 128
RWKV_HEAD_DIM = 64
GMLP_BLOCK = 128
STREAM_CHUNK = 64
SCAN_CHUNK = 64
VMEM_LIMIT = 56 * 1024 * 1024


def _cparams(*sem):
    return pltpu.CompilerParams(dimension_semantics=sem, vmem_limit_bytes=VMEM_LIMIT)


def _nt(a, b, precision=None):
    return lax.dot_general(a, b, (((1,), (1,)), ((), ())), precision=precision,
                           preferred_element_type=F32)


def _tn(a, b, precision=None):
    return lax.dot_general(a, b, (((0,), (0,)), ((), ())), precision=precision,
                           preferred_element_type=F32)


def _mm(a, b, precision=None):
    return jnp.dot(a, b, precision=precision, preferred_element_type=F32)


def _head_ones(n=V7X_LANES, head=RWKV_HEAD_DIM):
    r = lax.broadcasted_iota(jnp.int32, (n, n), 0) // head
    c = lax.broadcasted_iota(jnp.int32, (n, n), 1) // head
    return (r == c).astype(F32)


def _head_sum(x, ones_bd):
    w = x.shape[1]
    parts = [_mm(x[:, j:j + V7X_LANES], ones_bd, HIGHEST) for j in range(0, w, V7X_LANES)]
    return parts[0] if len(parts) == 1 else jnp.concatenate(parts, axis=1)


def _rms_kernel(x_ref, g_ref, o_ref):
    x = x_ref[...]
    ms = jnp.mean(x * x, axis=-1, keepdims=True)
    o_ref[...] = (x * lax.rsqrt(ms + RMS_EPS) * g_ref[...]).astype(o_ref.dtype)


def rms_norm_bf16(x, g, tm):
    n, d = x.shape
    return pl.pallas_call(
        _rms_kernel,
        grid=(n // tm,),
        in_specs=[pl.BlockSpec((tm, d), lambda i: (i, 0)), pl.BlockSpec((1, d), lambda i: (0, 0))],
        out_specs=pl.BlockSpec((tm, d), lambda i: (i, 0)),
        out_shape=jax.ShapeDtypeStruct((n, d), BF16),
        compiler_params=_cparams("parallel"),
        name="rms_norm",
    )(x, g.reshape(1, d))


def _mm_kernel(a_ref, w_ref, o_ref):
    o_ref[...] = _mm(a_ref[...], w_ref[...]).astype(o_ref.dtype)


def matmul(a, w, tm, tn, name):
    m, k = a.shape
    n = w.shape[1]
    return pl.pallas_call(
        _mm_kernel,
        grid=(m // tm, n // tn),
        in_specs=[pl.BlockSpec((tm, k), lambda i, j: (i, 0)), pl.BlockSpec((k, tn), lambda i, j: (0, j))],
        out_specs=pl.BlockSpec((tm, tn), lambda i, j: (i, j)),
        out_shape=jax.ShapeDtypeStruct((m, n), F32),
        compiler_params=_cparams("parallel", "arbitrary"),
        name=name,
    )(a, w)


def _mm2_kernel(a1_ref, a2_ref, w1_ref, w2_ref, o_ref):
    o_ref[...] = _mm(a1_ref[...], w1_ref[...]) + _mm(a2_ref[...], w2_ref[...])


def matmul_concat2(a1, a2, w, tm, tn, name):
    m, k1 = a1.shape
    k2 = a2.shape[1]
    n = w.shape[1]
    nb1 = 1
    assert k1 == k2
    return pl.pallas_call(
        _mm2_kernel,
        grid=(m // tm, n // tn),
        in_specs=[pl.BlockSpec((tm, k1), lambda i, j: (i, 0)),
                  pl.BlockSpec((tm, k2), lambda i, j: (i, 0)),
                  pl.BlockSpec((k1, tn), lambda i, j: (0, j)),
                  pl.BlockSpec((k2, tn), lambda i, j: (nb1, j))],
        out_specs=pl.BlockSpec((tm, tn), lambda i, j: (i, j)),
        out_shape=jax.ShapeDtypeStruct((m, n), F32),
        compiler_params=_cparams("parallel", "arbitrary"),
        name=name,
    )(a1, a2, w, w)


def _gmlp_kernel(u_ref, v_ref, lng_ref, lnb_ref, ws_ref, bs_ref, o_ref, *, heads, nblk):
    zv = jax.nn.gelu(v_ref[...])
    mean = jnp.mean(zv, axis=-1, keepdims=True)
    xc = zv - mean
    var = jnp.mean(xc * xc, axis=-1, keepdims=True)
    vn = (xc * lax.rsqrt(var + LN_EPS) * lng_ref[...] + lnb_ref[...]).astype(BF16)
    row = lax.broadcasted_iota(jnp.int32, (GMLP_BLOCK, GMLP_BLOCK), 0) // STREAM_CHUNK
    col = lax.broadcasted_iota(jnp.int32, (GMLP_BLOCK, GMLP_BLOCK), 1) // STREAM_CHUNK
    causal = col <= row
    for h in range(heads):
        wm = jnp.where(causal, ws_ref[h], 0.0).astype(BF16)
        bias = bs_ref[h]
        cs = slice(h * V7X_LANES, (h + 1) * V7X_LANES)
        for n in range(nblk):
            rs = slice(n * GMLP_BLOCK, (n + 1) * GMLP_BLOCK)
            mixed = _mm(wm, vn[rs, cs]) + bias
            o_ref[rs, cs] = (jax.nn.gelu(u_ref[rs, cs]) * mixed).astype(o_ref.dtype)


def gmlp_gating(p_main, ln_g, ln_b, ws, bs, rows):
    n = p_main.shape[0]
    heads = ws.shape[0]
    gw = heads * V7X_LANES
    assert ws.shape[1:] == (GMLP_BLOCK, GMLP_BLOCK) and rows % GMLP_BLOCK == 0
    kern = functools.partial(_gmlp_kernel, heads=heads, nblk=rows // GMLP_BLOCK)
    return pl.pallas_call(
        kern,
        grid=(n // rows,),
        in_specs=[pl.BlockSpec((rows, gw), lambda i: (i, 0)),
                  pl.BlockSpec((rows, gw), lambda i: (i, 1)),
                  pl.BlockSpec((1, gw), lambda i: (0, 0)),
                  pl.BlockSpec((1, gw), lambda i: (0, 0)),
                  pl.BlockSpec((heads, GMLP_BLOCK, GMLP_BLOCK), lambda i: (0, 0, 0)),
                  pl.BlockSpec((heads, GMLP_BLOCK, 1), lambda i: (0, 0, 0))],
        out_specs=pl.BlockSpec((rows, gw), lambda i: (i, 0)),
        out_shape=jax.ShapeDtypeStruct((n, gw), BF16),
        compiler_params=_cparams("parallel"),
        name="gmlp_gating",
    )(p_main, p_main, ln_g.reshape(1, gw), ln_b.reshape(1, gw), ws, bs[:, :, None])


def _rwkv_prep_kernel(r_ref, k_ref, v_ref, t_ref, rh_ref, kh_ref, vh_ref, th_ref,
                      mur_ref, muk_ref, muv_ref, mut_ref, w0_ref, wup_ref, a0_ref, aup_ref, gup_ref,
                      kk_ref, ka_ref, rk_ref,
                      ro_ref, ko_ref, vo_ref, lw_ref, kko_ref, bo_ref, g_ref, bv_ref, *, blocks_per_seq):
    first = (pl.program_id(0) % blocks_per_seq) == 0

    def shifted(ref, halo_ref, mu_ref):
        p = ref[...]
        halo = jnp.where(first, 0.0, halo_ref[7:8, :])
        rows = lax.broadcasted_iota(jnp.int32, p.shape, 0)
        prev = jnp.where(rows == 0, halo, pltpu.roll(p, 1, 0))
        return p + (prev - p) * mu_ref[...]

    r = shifted(r_ref, rh_ref, mur_ref)
    k = shifted(k_ref, kh_ref, muk_ref)
    v = shifted(v_ref, vh_ref, muv_ref)
    tail = shifted(t_ref, th_ref, mut_ref)
    lora_in = tail[:, :V7X_LANES]
    xg = tail[:, V7X_LANES:]

    zw = w0_ref[...] + _mm(jnp.tanh(lora_in).astype(BF16), wup_ref[...])
    w_log = -jax.nn.softplus(-zw) - 0.5
    lw_ref[...] = -jnp.exp(w_log)
    a = jax.nn.sigmoid(a0_ref[...] + _mm(lora_in.astype(BF16), aup_ref[...]))
    g_ref[...] = _mm(jax.nn.sigmoid(xg).astype(BF16), gup_ref[...])

    ones_bd = _head_ones()
    kk = k * kk_ref[...]
    kk = kk / jnp.maximum(jnp.sqrt(_head_sum(kk * kk, ones_bd)), L2_EPS)
    k2 = k * (1.0 + (a - 1.0) * ka_ref[...])
    bonus = _head_sum(r * k2 * rk_ref[...], ones_bd)
    ro_ref[...] = r
    ko_ref[...] = k2
    vo_ref[...] = v
    kko_ref[...] = kk
    bo_ref[...] = kk * a
    bv_ref[...] = bonus * v


def rwkv_prep(p_main, p_tail, mu, w0, w_up, a0, a_up, g_up, k_k, k_a, r_k, tm, seq, first_col_block):
    n = p_main.shape[0]
    rw = w0.shape[0]
    dl, il = w_up.shape[0], a_up.shape[0]
    assert dl + il == V7X_LANES and seq % tm == 0 and tm % 8 == 0
    tw = p_tail.shape[1]
    cb = first_col_block
    row = lambda a: a.reshape(1, -1)
    mur, muk, muv, mut = mu[:rw], mu[rw:2 * rw], mu[2 * rw:3 * rw], mu[3 * rw:]
    wup_pad = jnp.concatenate([w_up, jnp.zeros((il, rw), w_up.dtype)], axis=0).astype(BF16)
    aup_pad = jnp.concatenate([jnp.zeros((dl, rw), a_up.dtype), a_up], axis=0).astype(BF16)
    blk = lambda c: pl.BlockSpec((tm, rw), lambda i: (i, c))
    halo = lambda c: pl.BlockSpec((8, rw), lambda i: (jnp.maximum(i * (tm // 8) - 1, 0), c))
    vec = lambda w: pl.BlockSpec((1, w), lambda i: (0, 0))
    full = lambda a: pl.BlockSpec(a.shape, lambda i: (0, 0))
    gup = g_up.astype(BF16)
    out_blk = pl.BlockSpec((tm, rw), lambda i: (i, 0))
    outs = pl.pallas_call(
        functools.partial(_rwkv_prep_kernel, blocks_per_seq=seq // tm),
        grid=(n // tm,),
        in_specs=[blk(cb), blk(cb + 1), blk(cb + 2), pl.BlockSpec((tm, tw), lambda i: (i, 0)),
                  halo(cb), halo(cb + 1), halo(cb + 2),
                  pl.BlockSpec((8, tw), lambda i: (jnp.maximum(i * (tm // 8) - 1, 0), 0)),
                  vec(rw), vec(rw), vec(rw), vec(tw), vec(rw), full(wup_pad), vec(rw), full(aup_pad), full(gup),
                  vec(rw), vec(rw), vec(rw)],
        out_specs=[out_blk] * 8,
        out_shape=[jax.ShapeDtypeStruct((n, rw), F32)] * 8,
        compiler_params=_cparams("parallel"),
        name="rwkv_prep",
    )(p_main, p_main, p_main, p_tail, p_main, p_main, p_main, p_tail,
      row(mur), row(muk), row(muv), row(mut), row(w0), wup_pad, row(a0), aup_pad, gup,
      row(k_k), row(k_a), row(r_k))
    return outs


def _scan_kernel(r_ref, k_ref, v_ref, lw_ref, kk_ref, b_ref, y_ref, state_ref, *, heads, nchunks):
    C = SCAN_CHUNK
    N = RWKV_HEAD_DIM
    L = heads * N

    @pl.when(pl.program_id(2) == 0)
    def _():
        state_ref[...] = jnp.zeros_like(state_ref)

    ti = lax.broadcasted_iota(jnp.int32, (C, C), 0)
    si = lax.broadcasted_iota(jnp.int32, (C, C), 1)
    tri = (ti >= si).astype(F32)

    lane = lax.broadcasted_iota(jnp.int32, (C, L), 1)
    rowi = lax.broadcasted_iota(jnp.int32, (C, L), 0)
    lane_h = lane // N
    scol = lane % N
    strict = rowi > scol
    incl = rowi >= scol
    eye_p = (rowi == scol).astype(F32)
    r2 = lax.broadcasted_iota(jnp.int32, (L, L), 0)
    c2 = lax.broadcasted_iota(jnp.int32, (L, L), 1)
    bd_mask = (r2 // N) == (c2 // N)
    eye2 = r2 == c2
    head_sel = [lane_h == h for h in range(heads)]

    def sm(x):
        xb = x.astype(BF16)
        zero = jnp.zeros_like(xb)
        return jnp.concatenate([jnp.where(sel, xb, zero) for sel in head_sel], axis=0)

    def mmb(a, b):
        return _mm(a.astype(BF16), b.astype(BF16))

    chunks = range(nchunks)

    def each(fn, *lists):
        return [fn(*xs) for xs in zip(*lists)]

    rows = [slice(c * C, (c + 1) * C) for c in chunks]
    lw = [lw_ref[rs, :] for rs in rows]
    cm = [_mm(tri, x, HIGHEST) for x in lw]
    r = [r_ref[rs, :] for rs in rows]
    k = [k_ref[rs, :] for rs in rows]
    v = [v_ref[rs, :] for rs in rows]
    kk = [kk_ref[rs, :] for rs in rows]
    beta = [b_ref[rs, :] for rs in rows]
    ginv = each(lambda m: jnp.exp(-m), cm)
    a_t = each(lambda kk_, m, w: -kk_ * jnp.exp(m - w), kk, cm, lw)
    b_t = each(jnp.multiply, beta, ginv)
    k_t = each(jnp.multiply, k, ginv)
    r_t = each(lambda r_, m: r_ * jnp.exp(m), r, cm)
    dec_end = each(lambda m: jnp.exp(jnp.broadcast_to(m[C - 1:C, :], (C, L)) - m), cm)
    b_h = each(jnp.multiply, beta, dec_end)
    k_h = each(jnp.multiply, k, dec_end)

    sc = each(lambda a_, r_, b_, k_: _nt(jnp.concatenate([a_, r_], axis=0).astype(BF16),
                                         jnp.concatenate([sm(b_), sm(k_)], axis=0)),
              a_t, r_t, b_t, k_t)
    ab = [x[:C, :L] for x in sc]
    l_ak = [jnp.where(strict, x[:C, L:], 0.0) for x in sc]
    p_rb = [jnp.where(incl, x[C:, :L], 0.0) for x in sc]
    p_rk = [jnp.where(incl, x[C:, L:], 0.0) for x in sc]

    base = 8
    d = [jnp.where(strict & (rowi // base == scol // base), x, 0.0) for x in ab]
    t_inv = [eye_p + x for x in d]
    p2 = each(lambda x: mmb(x, sm(x)), d)
    both = each(lambda p, t: mmb(p, jnp.concatenate([sm(p), sm(t)], axis=1)), p2, t_inv)
    t_inv = each(lambda t, bo: t + bo[:, L:], t_inv, both)
    t_inv = each(lambda t, bo: t + mmb(bo[:, :L], sm(t)), t_inv, both)
    b = base
    while b < C:
        blk_off = (rowi // (2 * b) == scol // (2 * b)) & (rowi // b > scol // b)
        ot = each(lambda x, t: mmb(jnp.where(blk_off, x, 0.0), sm(t)), ab, t_inv)
        t_inv = each(lambda t, o: t + mmb(t, sm(o)), t_inv, ot)
        b *= 2

    sm_v = each(sm, v)
    x1 = each(mmb, l_ak, sm_v)
    wv = each(lambda t, a_, x_: mmb(t, jnp.concatenate([sm(a_), sm(x_)], axis=1)), t_inv, a_t, x1)
    w_t = [x[:, :L] for x in wv]
    v_t = [x[:, L:] for x in wv]
    q = each(lambda r_, p, w: (r_ + mmb(p, sm(w))).astype(BF16), r_t, p_rb, w_t)
    y0 = each(lambda pb, pk, vt, sv: mmb(jnp.concatenate([pb, pk], axis=1), jnp.concatenate([sm(vt), sv], axis=0)),
              p_rb, p_rk, v_t, sm_v)
    b_ht = [x.T for x in b_h]
    k_ht = [x.T for x in k_h]
    m_bd = each(lambda bt, w: jnp.where(bd_mask, mmb(bt, w), 0.0).astype(BF16), b_ht, w_t)
    n_bd = each(lambda bt, kt, vt, v_: jnp.where(bd_mask, mmb(jnp.concatenate([bt, kt], axis=1),
                                                               jnp.concatenate([vt, v_], axis=0)), 0.0),
                b_ht, k_ht, v_t, v)
    gcol = each(lambda m: jnp.sum(jnp.where(eye2, jnp.exp(jnp.broadcast_to(m[C - 1:C, :], (L, L))), 0.0),
                                  axis=1, keepdims=True), cm)

    S = state_ref[...]
    for c in chunks:
        sb = S.astype(BF16)
        y_ref[rows[c], :] = _mm(q[c], sb) + y0[c]
        S = gcol[c] * S + _mm(m_bd[c], sb) + n_bd[c]
    state_ref[...] = S


def rwkv_scan(r, k, v, lw, kk, beta, batch, seq, tc, heads):
    n, rw = r.shape
    L = heads * RWKV_HEAD_DIM
    assert seq % tc == 0 and tc % SCAN_CHUNK == 0 and rw % L == 0 and L % V7X_LANES == 0
    nt = seq // tc
    blk = pl.BlockSpec((tc, L), lambda b, g, c: (b * nt + c, g))
    return pl.pallas_call(
        functools.partial(_scan_kernel, heads=heads, nchunks=tc // SCAN_CHUNK),
        grid=(batch, rw // L, nt),
        in_specs=[blk] * 6,
        out_specs=blk,
        out_shape=jax.ShapeDtypeStruct((n, rw), F32),
        scratch_shapes=[pltpu.VMEM((L, L), F32)],
        compiler_params=_cparams("parallel", "parallel", "arbitrary"),
        name="rwkv_scan",
    )(r, k, v, lw, kk, beta)


def _rwkv_post_kernel(y_ref, bv_ref, g_ref, lg_ref, lb_ref, o_ref):
    ones_bd = _head_ones()
    y = y_ref[...]
    inv_n = 1.0 / RWKV_HEAD_DIM
    yc = y - _head_sum(y, ones_bd) * inv_n
    var = _head_sum(yc * yc, ones_bd) * inv_n
    yn = yc * lax.rsqrt(var + RWKV_HEAD_DIM * GN_EPS_PER_CH) * lg_ref[...] + lb_ref[...]
    o_ref[...] = ((yn + bv_ref[...]) * g_ref[...]).astype(o_ref.dtype)


def rwkv_post(y, bv, g, lnx_g, lnx_b, tm):
    n, rw = y.shape
    blk = pl.BlockSpec((tm, rw), lambda i: (i, 0))
    vec = pl.BlockSpec((1, rw), lambda i: (0, 0))
    return pl.pallas_call(
        _rwkv_post_kernel,
        grid=(n // tm,),
        in_specs=[blk, blk, blk, vec, vec],
        out_specs=blk,
        out_shape=jax.ShapeDtypeStruct((n, rw), BF16),
        compiler_params=_cparams("parallel"),
        name="rwkv_post",
    )(y, bv, g, lnx_g.reshape(1, rw), lnx_b.reshape(1, rw))


def _post_mix_kernel(mix_ref, x_ref, gpost_ref, gpre_ref, x1_ref, xn_ref):
    m = mix_ref[...]
    x1 = x_ref[...] + m * lax.rsqrt(jnp.mean(m * m, axis=-1, keepdims=True) + RMS_EPS) * gpost_ref[...]
    x1_ref[...] = x1
    xn_ref[...] = (x1 * lax.rsqrt(jnp.mean(x1 * x1, axis=-1, keepdims=True) + RMS_EPS)
                   * gpre_ref[...]).astype(xn_ref.dtype)


def post_mix(mix, x, g_post, g_pre, tm):
    n, d = x.shape
    blk = pl.BlockSpec((tm, d), lambda i: (i, 0))
    vec = pl.BlockSpec((1, d), lambda i: (0, 0))
    return pl.pallas_call(
        _post_mix_kernel,
        grid=(n // tm,),
        in_specs=[blk, blk, vec, vec],
        out_specs=[blk, blk],
        out_shape=[jax.ShapeDtypeStruct((n, d), F32), jax.ShapeDtypeStruct((n, d), BF16)],
        compiler_params=_cparams("parallel"),
        name="post_mix",
    )(mix, x, g_post.reshape(1, d), g_pre.reshape(1, d))


def _post_ffn_kernel(f_ref, x_ref, g_ref, o_ref):
    f = f_ref[...]
    o_ref[...] = x_ref[...] + f * lax.rsqrt(jnp.mean(f * f, axis=-1, keepdims=True) + RMS_EPS) * g_ref[...]


def post_ffn(f, x1, g, tm):
    n, d = x1.shape
    blk = pl.BlockSpec((tm, d), lambda i: (i, 0))
    return pl.pallas_call(
        _post_ffn_kernel,
        grid=(n // tm,),
        in_specs=[blk, blk, pl.BlockSpec((1, d), lambda i: (0, 0))],
        out_specs=blk,
        out_shape=jax.ShapeDtypeStruct((n, d), F32),
        compiler_params=_cparams("parallel"),
        name="post_ffn",
    )(f, x1, g.reshape(1, d))


def _ffn_kernel(xn_ref, w1_ref, w2_ref, o_ref):
    @pl.when(pl.program_id(1) == 0)
    def _():
        o_ref[...] = jnp.zeros_like(o_ref)

    h = jnp.maximum(_mm(xn_ref[...], w1_ref[...]), 0.0)
    o_ref[...] += _mm((h * h).astype(BF16), w2_ref[...])


def ffn(xn, w1, w2, tm, tf):
    n, d = xn.shape
    f = w1.shape[1]
    return pl.pallas_call(
        _ffn_kernel,
        grid=(n // tm, f // tf),
        in_specs=[pl.BlockSpec((tm, d), lambda i, j: (i, 0)),
                  pl.BlockSpec((d, tf), lambda i, j: (0, j)),
                  pl.BlockSpec((tf, d), lambda i, j: (j, 0))],
        out_specs=pl.BlockSpec((tm, d), lambda i, j: (i, 0)),
        out_shape=jax.ShapeDtypeStruct((n, d), F32),
        compiler_params=_cparams("parallel", "arbitrary"),
        name="ffn",
    )(xn, w1, w2)


def _tile(n, want):
    if n <= want:
        return n
    t = want - want % V7X_LANES
    while n % t:
        t -= V7X_LANES
    assert t > 0, (n, want)
    return t


def kernel(x, pre_mix_g, w_in, tshift_mu, gmlp_ln_g, gmlp_ln_b, gmlp_ws, gmlp_bs, decay_w0, decay_up, iclr_a0, iclr_up, gate_up, k_k, k_a, r_k, lnx_g, lnx_b, w_out, post_mix_g, pre_ffn_g, w_ff1, w_ff2, post_ffn_g):
    batch, seq, d = x.shape
    depth = w_in.shape[0]
    n = batch * seq
    gw = gmlp_ws.shape[1] * V7X_LANES
    rw = decay_w0.shape[1]
    main_w = 2 * gw + 3 * rw
    assert gw == rw, "column-block indexing below assumes equal head-group widths"
    xf = x.reshape(n, d)
    for l in range(depth):
        w_in_main = w_in[l, :, :main_w].astype(BF16)
        w_in_tail = w_in[l, :, main_w:].astype(BF16)
        h = rms_norm_bf16(xf, pre_mix_g[l], _tile(n, 512))
        p_main = matmul(h, w_in_main, _tile(n, 1024), _tile(main_w, 1024), "w_in_main")
        p_tail = matmul(h, w_in_tail, _tile(n, 1024), w_in_tail.shape[1], "w_in_tail")

        y_a = gmlp_gating(p_main, gmlp_ln_g[l], gmlp_ln_b[l], gmlp_ws[l], gmlp_bs[l], _tile(seq, 512))

        r, k2, v, lw, kk, beta, g, bv = rwkv_prep(
            p_main, p_tail, tshift_mu[l], decay_w0[l], decay_up[l], iclr_a0[l], iclr_up[l], gate_up[l],
            k_k[l], k_a[l], r_k[l].reshape(-1), _tile(seq, 128), seq, first_col_block=2)
        y = rwkv_scan(r, k2, v, lw, kk, beta, batch, seq, _tile(seq, 512), heads=4)
        y_b = rwkv_post(y, bv, g, lnx_g[l], lnx_b[l], _tile(n, 512))

        mix = matmul_concat2(y_a, y_b, w_out[l].astype(BF16), _tile(n, 1024), _tile(d, 1024), "w_out")
        x1, xn = post_mix(mix, xf, post_mix_g[l], pre_ffn_g[l], _tile(n, 256))
        f = ffn(xn, w_ff1[l].astype(BF16), w_ff2[l].astype(BF16), _tile(n, 512), _tile(w_ff1.shape[2], 512))
        xf = post_ffn(f, x1, post_ffn_g[l], _tile(n, 256))
    return xf.reshape(batch, seq, d)
```

```python
import functools

import jax
import jax.numpy as jnp
from jax import lax
from jax.experimental import pallas as pl
from jax.experimental.pallas import tpu as pltpu

F32 = jnp.float32
BF16 = jnp.bfloat16
HIGHEST = lax.Precision.HIGHEST

RMS_EPS = 1e-6
LN_EPS = 1e-5
L2_EPS = 1e-12
GN_EPS_PER_CH = 1e-5

V7X_LANES = 128
V7X_SUBLANES = 8
RWKV_HEAD_DIM = 64
GMLP_BLOCK = 128
STREAM_CHUNK = 64
SCAN_CHUNK = 64
SCAN_HEADS = 4
VMEM_LIMIT = 56 * 1024 * 1024


def _cparams(*sem):
    return pltpu.CompilerParams(dimension_semantics=sem, vmem_limit_bytes=VMEM_LIMIT)


def _nt(a, b):
    return lax.dot_general(a, b, (((1,), (1,)), ((), ())), preferred_element_type=F32)


def _mm(a, b, precision=None):
    return jnp.dot(a, b, precision=precision, preferred_element_type=F32)


def _mmb(a, b):
    return _mm(a.astype(BF16), b.astype(BF16))


def _rms_kernel(x_ref, g_ref, o_ref):
    x = x_ref[...]
    ms = jnp.mean(x * x, axis=-1, keepdims=True)
    o_ref[...] = (x * lax.rsqrt(ms + RMS_EPS) * g_ref[...]).astype(o_ref.dtype)


def rms_norm_bf16(x, g, tm):
    n, d = x.shape
    return pl.pallas_call(
        _rms_kernel,
        grid=(n // tm,),
        in_specs=[pl.BlockSpec((tm, d), lambda i: (i, 0)), pl.BlockSpec((1, d), lambda i: (0, 0))],
        out_specs=pl.BlockSpec((tm, d), lambda i: (i, 0)),
        out_shape=jax.ShapeDtypeStruct((n, d), BF16),
        compiler_params=_cparams("parallel"),
        name="rms_norm",
    )(x, g.reshape(1, d))


def _mm_kernel(a_ref, w_ref, o_ref):
    o_ref[...] = _mm(a_ref[...], w_ref[...]).astype(o_ref.dtype)


def matmul(a, w, tm, tn, name):
    m, k = a.shape
    n = w.shape[1]
    return pl.pallas_call(
        _mm_kernel,
        grid=(m // tm, n // tn),
        in_specs=[pl.BlockSpec((tm, k), lambda i, j: (i, 0)), pl.BlockSpec((k, tn), lambda i, j: (0, j))],
        out_specs=pl.BlockSpec((tm, tn), lambda i, j: (i, j)),
        out_shape=jax.ShapeDtypeStruct((m, n), F32),
        compiler_params=_cparams("parallel", "arbitrary"),
        name=name,
    )(a, w)


def _mm2_kernel(a1_ref, a2_ref, w1_ref, w2_ref, o_ref):
    o_ref[...] = _mm(a1_ref[...], w1_ref[...]) + _mm(a2_ref[...], w2_ref[...])


def matmul_concat2(a1, a2, w, tm, tn, name):
    m, k1 = a1.shape
    k2 = a2.shape[1]
    n = w.shape[1]
    assert k1 == k2
    return pl.pallas_call(
        _mm2_kernel,
        grid=(m // tm, n // tn),
        in_specs=[pl.BlockSpec((tm, k1), lambda i, j: (i, 0)),
                  pl.BlockSpec((tm, k2), lambda i, j: (i, 0)),
                  pl.BlockSpec((k1, tn), lambda i, j: (0, j)),
                  pl.BlockSpec((k2, tn), lambda i, j: (1, j))],
        out_specs=pl.BlockSpec((tm, tn), lambda i, j: (i, j)),
        out_shape=jax.ShapeDtypeStruct((m, n), F32),
        compiler_params=_cparams("parallel", "arbitrary"),
        name=name,
    )(a1, a2, w, w)


def _gmlp_kernel(u_ref, v_ref, lng_ref, lnb_ref, ws_ref, bs_ref, o_ref, *, heads, nblk):
    zv = jax.nn.gelu(v_ref[...])
    mean = jnp.mean(zv, axis=-1, keepdims=True)
    xc = zv - mean
    var = jnp.mean(xc * xc, axis=-1, keepdims=True)
    vn = (xc * lax.rsqrt(var + LN_EPS) * lng_ref[...] + lnb_ref[...]).astype(BF16)
    row = lax.broadcasted_iota(jnp.int32, (GMLP_BLOCK, GMLP_BLOCK), 0) // STREAM_CHUNK
    col = lax.broadcasted_iota(jnp.int32, (GMLP_BLOCK, GMLP_BLOCK), 1) // STREAM_CHUNK
    causal = col <= row
    for h in range(heads):
        wm = jnp.where(causal, ws_ref[h], 0.0).astype(BF16)
        bias = bs_ref[h]
        cs = slice(h * V7X_LANES, (h + 1) * V7X_LANES)
        for n in range(nblk):
            rs = slice(n * GMLP_BLOCK, (n + 1) * GMLP_BLOCK)
            mixed = _mm(wm, vn[rs, cs]) + bias
            o_ref[rs, cs] = (jax.nn.gelu(u_ref[rs, cs]) * mixed).astype(o_ref.dtype)


def gmlp_gating(p_main, ln_g, ln_b, ws, bs, rows):
    n = p_main.shape[0]
    heads = ws.shape[0]
    gw = heads * V7X_LANES
    assert ws.shape[1:] == (GMLP_BLOCK, GMLP_BLOCK) and rows % GMLP_BLOCK == 0
    kern = functools.partial(_gmlp_kernel, heads=heads, nblk=rows // GMLP_BLOCK)
    return pl.pallas_call(
        kern,
        grid=(n // rows,),
        in_specs=[pl.BlockSpec((rows, gw), lambda i: (i, 0)),
                  pl.BlockSpec((rows, gw), lambda i: (i, 1)),
                  pl.BlockSpec((1, gw), lambda i: (0, 0)),
                  pl.BlockSpec((1, gw), lambda i: (0, 0)),
                  pl.BlockSpec((heads, GMLP_BLOCK, GMLP_BLOCK), lambda i: (0, 0, 0)),
                  pl.BlockSpec((heads, GMLP_BLOCK, 1), lambda i: (0, 0, 0))],
        out_specs=pl.BlockSpec((rows, gw), lambda i: (i, 0)),
        out_shape=jax.ShapeDtypeStruct((n, gw), BF16),
        compiler_params=_cparams("parallel"),
        name="gmlp_gating",
    )(p_main, p_main, ln_g.reshape(1, gw), ln_b.reshape(1, gw), ws, bs[:, :, None])


def _rwkv_kernel(r_ref, k_ref, v_ref, t_ref, rh_ref, kh_ref, vh_ref, th_ref,
                 mur_ref, muk_ref, muv_ref, mut_ref, w0_ref, wup_ref, a0_ref, aup_ref, gup_ref,
                 kk_ref, ka_ref, rk_ref, lng_ref, lnb_ref, o_ref, state_ref, *, heads, nchunks):
    C = SCAN_CHUNK
    N = RWKV_HEAD_DIM
    L = heads * N
    first = pl.program_id(2) == 0

    @pl.when(first)
    def _():
        state_ref[...] = jnp.zeros_like(state_ref)

    def shifted(ref, halo_ref, mu_ref):
        p = ref[...]
        halo = jnp.where(first, 0.0, halo_ref[V7X_SUBLANES - 1:V7X_SUBLANES, :])
        rows = lax.broadcasted_iota(jnp.int32, p.shape, 0)
        prev = jnp.where(rows == 0, halo, pltpu.roll(p, 1, 0))
        return p + (prev - p) * mu_ref[...]

    r_all = shifted(r_ref, rh_ref, mur_ref)
    k_all = shifted(k_ref, kh_ref, muk_ref)
    v_all = shifted(v_ref, vh_ref, muv_ref)
    tail = shifted(t_ref, th_ref, mut_ref)
    lora_in = tail[:, :V7X_LANES]
    xg = tail[:, V7X_LANES:]

    zw = w0_ref[...] + _mm(jnp.tanh(lora_in).astype(BF16), wup_ref[...])
    lw_all = -jnp.exp(-jax.nn.softplus(-zw) - 0.5)
    a_all = jax.nn.sigmoid(a0_ref[...] + _mm(lora_in.astype(BF16), aup_ref[...]))
    g_all = _mm(jax.nn.sigmoid(xg).astype(BF16), gup_ref[...])

    r2 = lax.broadcasted_iota(jnp.int32, (L, L), 0)
    c2 = lax.broadcasted_iota(jnp.int32, (L, L), 1)
    bd_mask = (r2 // N) == (c2 // N)
    eye2 = r2 == c2
    ones_bd = bd_mask.astype(BF16)

    def head_sum(x):
        hi = x.astype(BF16)
        lo = (x - hi.astype(F32)).astype(BF16)
        return _mm(hi, ones_bd) + _mm(lo, ones_bd)

    kk_all = k_all * kk_ref[...]
    kk_all = kk_all / jnp.maximum(jnp.sqrt(head_sum(kk_all * kk_all)), L2_EPS)
    k2_all = k_all * (1.0 + (a_all - 1.0) * ka_ref[...])
    bv_all = head_sum(r_all * k2_all * rk_ref[...]) * v_all
    beta_all = kk_all * a_all

    ti = lax.broadcasted_iota(jnp.int32, (C, C), 0)
    si = lax.broadcasted_iota(jnp.int32, (C, C), 1)
    tri = (ti >= si).astype(F32)
    lane = lax.broadcasted_iota(jnp.int32, (C, L), 1)
    rowi = lax.broadcasted_iota(jnp.int32, (C, L), 0)
    scol = lane % N
    strict = rowi > scol
    incl = rowi >= scol
    eye_p = (rowi == scol).astype(F32)
    head_sel = [lane // N == h for h in range(heads)]

    def sm(x):
        xb = x.astype(BF16)
        zero = jnp.zeros_like(xb)
        return jnp.concatenate([jnp.where(sel, xb, zero) for sel in head_sel], axis=0)

    chunks = range(nchunks)

    def each(fn, *lists):
        return [fn(*xs) for xs in zip(*lists)]

    rows = [slice(c * C, (c + 1) * C) for c in chunks]
    lw = [lw_all[rs] for rs in rows]
    cm = [_mm(tri, x, HIGHEST) for x in lw]
    r = [r_all[rs] for rs in rows]
    k = [k2_all[rs] for rs in rows]
    v = [v_all[rs] for rs in rows]
    kk = [kk_all[rs] for rs in rows]
    beta = [beta_all[rs] for rs in rows]
    ginv = each(lambda m: jnp.exp(-m), cm)
    a_t = each(lambda kk_, m, w: -kk_ * jnp.exp(m - w), kk, cm, lw)
    b_t = each(jnp.multiply, beta, ginv)
    k_t = each(jnp.multiply, k, ginv)
    r_t = each(lambda r_, m: r_ * jnp.exp(m), r, cm)
    dec_end = each(lambda m: jnp.exp(jnp.broadcast_to(m[C - 1:C, :], (C, L)) - m), cm)
    b_h = each(jnp.multiply, beta, dec_end)
    k_h = each(jnp.multiply, k, dec_end)

    sc = each(lambda a_, r_, b_, k_: _nt(jnp.concatenate([a_, r_], axis=0).astype(BF16),
                                         jnp.concatenate([sm(b_), sm(k_)], axis=0)),
              a_t, r_t, b_t, k_t)
    ab = [x[:C, :L] for x in sc]
    l_ak = [jnp.where(strict, x[:C, L:], 0.0) for x in sc]
    p_rb = [jnp.where(incl, x[C:, :L], 0.0) for x in sc]
    p_rk = [jnp.where(incl, x[C:, L:], 0.0) for x in sc]

    base = 8
    d = [jnp.where(strict & (rowi // base == scol // base), x, 0.0) for x in ab]
    t_inv = [eye_p + x for x in d]
    p2 = each(lambda x: _mmb(x, sm(x)), d)
    both = each(lambda p, t: _mmb(p, jnp.concatenate([sm(p), sm(t)], axis=1)), p2, t_inv)
    t_inv = each(lambda t, bo: t + bo[:, L:], t_inv, both)
    t_inv = each(lambda t, bo: t + _mmb(bo[:, :L], sm(t)), t_inv, both)
    b = base
    while b < C:
        blk_off = (rowi // (2 * b) == scol // (2 * b)) & (rowi // b > scol // b)
        ot = each(lambda x, t: _mmb(jnp.where(blk_off, x, 0.0), sm(t)), ab, t_inv)
        t_inv = each(lambda t, o: t + _mmb(t, sm(o)), t_inv, ot)
        b *= 2

    sm_v = each(sm, v)
    x1 = each(_mmb, l_ak, sm_v)
    wv = each(lambda t, a_, x_: _mmb(t, jnp.concatenate([sm(a_), sm(x_)], axis=1)), t_inv, a_t, x1)
    w_t = [x[:, :L] for x in wv]
    v_t = [x[:, L:] for x in wv]
    q = each(lambda r_, p, w: (r_ + _mmb(p, sm(w))).astype(BF16), r_t, p_rb, w_t)
    y0 = each(lambda pb, pk, vt, sv: _mmb(jnp.concatenate([pb, pk], axis=1), jnp.concatenate([sm(vt), sv], axis=0)),
              p_rb, p_rk, v_t, sm_v)
    b_ht = [x.T for x in b_h]
    k_ht = [x.T for x in k_h]
    m_bd = each(lambda bt, w: jnp.where(bd_mask, _mmb(bt, w), 0.0).astype(BF16), b_ht, w_t)
    n_bd = each(lambda bt, kt, vt, v_: jnp.where(bd_mask, _mmb(jnp.concatenate([bt, kt], axis=1),
                                                                jnp.concatenate([vt, v_], axis=0)), 0.0),
                b_ht, k_ht, v_t, v)
    gcol = each(lambda m: jnp.sum(jnp.where(eye2, jnp.exp(jnp.broadcast_to(m[C - 1:C, :], (L, L))), 0.0),
                                  axis=1, keepdims=True), cm)

    S = state_ref[...]
    ys = []
    for c in chunks:
        sb = S.astype(BF16)
        ys.append(_mm(q[c], sb) + y0[c])
        S = gcol[c] * S + _mm(m_bd[c], sb) + n_bd[c]
    state_ref[...] = S

    y = jnp.concatenate(ys, axis=0)
    inv_n = 1.0 / N
    yc = y - head_sum(y) * inv_n
    var = head_sum(yc * yc) * inv_n
    yn = yc * lax.rsqrt(var + N * GN_EPS_PER_CH) * lng_ref[...] + lnb_ref[...]
    o_ref[...] = ((yn + bv_all) * g_all).astype(o_ref.dtype)


def rwkv_time_mix(p_main, p_tail, mu, w0, w_up, a0, a_up, g_up, k_k, k_a, r_k, lnx_g, lnx_b,
                  batch, seq, tc, first_col):
    n = p_main.shape[0]
    rw = w0.shape[0]
    L = SCAN_HEADS * RWKV_HEAD_DIM
    dl, il = w_up.shape[0], a_up.shape[0]
    tw = p_tail.shape[1]
    assert dl + il == V7X_LANES and seq % tc == 0 and tc % SCAN_CHUNK == 0 and rw % L == 0 and first_col % L == 0
    nt = seq // tc
    ng = rw // L
    cb = first_col // L
    sub = V7X_SUBLANES
    row = lambda a: a.reshape(1, -1)
    mur, muk, muv, mut = mu[:rw], mu[rw:2 * rw], mu[2 * rw:3 * rw], mu[3 * rw:]
    wup_pad = jnp.concatenate([w_up, jnp.zeros((il, rw), w_up.dtype)], axis=0).astype(BF16)
    aup_pad = jnp.concatenate([jnp.zeros((dl, rw), a_up.dtype), a_up], axis=0).astype(BF16)
    gup = g_up.astype(BF16)

    def halo_row(b, c):
        return jnp.maximum((b * nt + c) * (tc // sub) - 1, 0)

    blk = lambda i: pl.BlockSpec((tc, L), lambda b, g, c: (b * nt + c, cb + i * ng + g))
    halo = lambda i: pl.BlockSpec((sub, L), lambda b, g, c: (halo_row(b, c), cb + i * ng + g))
    vec = pl.BlockSpec((1, L), lambda b, g, c: (0, g))
    lora = lambda a: pl.BlockSpec((a.shape[0], L), lambda b, g, c: (0, g))
    return pl.pallas_call(
        functools.partial(_rwkv_kernel, heads=SCAN_HEADS, nchunks=tc // SCAN_CHUNK),
        grid=(batch, ng, nt),
        in_specs=[blk(0), blk(1), blk(2), pl.BlockSpec((tc, tw), lambda b, g, c: (b * nt + c, 0)),
                  halo(0), halo(1), halo(2), pl.BlockSpec((sub, tw), lambda b, g, c: (halo_row(b, c), 0)),
                  vec, vec, vec, pl.BlockSpec((1, tw), lambda b, g, c: (0, 0)),
                  vec, lora(wup_pad), vec, lora(aup_pad), lora(gup), vec, vec, vec, vec, vec],
        out_specs=pl.BlockSpec((tc, L), lambda b, g, c: (b * nt + c, g)),
        out_shape=jax.ShapeDtypeStruct((n, rw), BF16),
        scratch_shapes=[pltpu.VMEM((L, L), F32)],
        compiler_params=_cparams("parallel", "parallel", "arbitrary"),
        name="rwkv_time_mix",
    )(p_main, p_main, p_main, p_tail, p_main, p_main, p_main, p_tail,
      row(mur), row(muk), row(muv), row(mut), row(w0), wup_pad, row(a0), aup_pad, gup,
      row(k_k), row(k_a), row(r_k), row(lnx_g), row(lnx_b))


def _post_mix_kernel(mix_ref, x_ref, gpost_ref, gpre_ref, x1_ref, xn_ref):
    m = mix_ref[...]
    x1 = x_ref[...] + m * lax.rsqrt(jnp.mean(m * m, axis=-1, keepdims=True) + RMS_EPS) * gpost_ref[...]
    x1_ref[...] = x1
    xn_ref[...] = (x1 * lax.rsqrt(jnp.mean(x1 * x1, axis=-1, keepdims=True) + RMS_EPS)
                   * gpre_ref[...]).astype(xn_ref.dtype)


def post_mix(mix, x, g_post, g_pre, tm):
    n, d = x.shape
    blk = pl.BlockSpec((tm, d), lambda i: (i, 0))
    vec = pl.BlockSpec((1, d), lambda i: (0, 0))
    return pl.pallas_call(
        _post_mix_kernel,
        grid=(n // tm,),
        in_specs=[blk, blk, vec, vec],
        out_specs=[blk, blk],
        out_shape=[jax.ShapeDtypeStruct((n, d), F32), jax.ShapeDtypeStruct((n, d), BF16)],
        compiler_params=_cparams("parallel"),
        name="post_mix",
    )(mix, x, g_post.reshape(1, d), g_pre.reshape(1, d))


def _ffn_kernel(xn_ref, w1_ref, w2_ref, x1_ref, g_ref, o_ref):
    @pl.when(pl.program_id(1) == 0)
    def _():
        o_ref[...] = jnp.zeros_like(o_ref)

    h = jnp.maximum(_mm(xn_ref[...], w1_ref[...]), 0.0)
    o_ref[...] += _mm((h * h).astype(BF16), w2_ref[...])

    @pl.when(pl.program_id(1) == pl.num_programs(1) - 1)
    def _():
        f = o_ref[...]
        o_ref[...] = x1_ref[...] + f * lax.rsqrt(jnp.mean(f * f, axis=-1, keepdims=True) + RMS_EPS) * g_ref[...]


def ffn(xn, w1, w2, x1, g, tm, tf):
    n, d = xn.shape
    f = w1.shape[1]
    return pl.pallas_call(
        _ffn_kernel,
        grid=(n // tm, f // tf),
        in_specs=[pl.BlockSpec((tm, d), lambda i, j: (i, 0), pipeline_mode=pl.Buffered(1)),
                  pl.BlockSpec((d, tf), lambda i, j: (0, j)),
                  pl.BlockSpec((tf, d), lambda i, j: (j, 0)),
                  pl.BlockSpec((tm, d), lambda i, j: (i, 0), pipeline_mode=pl.Buffered(1)),
                  pl.BlockSpec((1, d), lambda i, j: (0, 0))],
        out_specs=pl.BlockSpec((tm, d), lambda i, j: (i, 0)),
        out_shape=jax.ShapeDtypeStruct((n, d), F32),
        compiler_params=_cparams("parallel", "arbitrary"),
        name="ffn",
    )(xn, w1, w2, x1, g.reshape(1, d))


def _tile(n, want):
    if n <= want:
        return n
    t = want - want % V7X_LANES
    while n % t:
        t -= V7X_LANES
    assert t > 0, (n, want)
    return t


def kernel(x, pre_mix_g, w_in, tshift_mu, gmlp_ln_g, gmlp_ln_b, gmlp_ws, gmlp_bs, decay_w0, decay_up, iclr_a0, iclr_up, gate_up, k_k, k_a, r_k, lnx_g, lnx_b, w_out, post_mix_g, pre_ffn_g, w_ff1, w_ff2, post_ffn_g):
    batch, seq, d = x.shape
    depth = w_in.shape[0]
    n = batch * seq
    gw = gmlp_ws.shape[1] * V7X_LANES
    rw = decay_w0.shape[1]
    main_w = 2 * gw + 3 * rw
    xf = x.reshape(n, d)
    for l in range(depth):
        w_in_main = w_in[l, :, :main_w].astype(BF16)
        w_in_tail = w_in[l, :, main_w:].astype(BF16)
        h = rms_norm_bf16(xf, pre_mix_g[l], _tile(n, 512))
        p_main = matmul(h, w_in_main, _tile(n, 1024), _tile(main_w, 1024), "w_in_main")
        p_tail = matmul(h, w_in_tail, _tile(n, 1024), w_in_tail.shape[1], "w_in_tail")

        y_a = gmlp_gating(p_main, gmlp_ln_g[l], gmlp_ln_b[l], gmlp_ws[l], gmlp_bs[l], _tile(seq, 512))
        y_b = rwkv_time_mix(p_main, p_tail, tshift_mu[l], decay_w0[l], decay_up[l], iclr_a0[l], iclr_up[l],
                            gate_up[l], k_k[l], k_a[l], r_k[l].reshape(-1), lnx_g[l], lnx_b[l],
                            batch, seq, _tile(seq, 512), first_col=2 * gw)

        mix = matmul_concat2(y_a, y_b, w_out[l].astype(BF16), _tile(n, 1024), _tile(d, 1024), "w_out")
        x1, xn = post_mix(mix, xf, post_mix_g[l], pre_ffn_g[l], _tile(n, 256))
        xf = ffn(xn, w_ff1[l].astype(BF16), w_ff2[l].astype(BF16), x1, post_ffn_g[l],
                 _tile(n, 512), _tile(w_ff1.shape[2], 512))
    return xf.reshape(batch, seq, d)
```

```python
import functools
import math

import jax
import jax.numpy as jnp
from jax import lax
from jax.experimental import pallas as pl
from jax.experimental.pallas import tpu as pltpu

F32 = jnp.float32
BF16 = jnp.bfloat16
HIGHEST = lax.Precision.HIGHEST

RMS_EPS = 1e-6
LN_EPS = 1e-5
L2_EPS = 1e-12
GN_EPS_PER_CH = 1e-5
DECAY_LOG_SCALE = math.exp(-0.5)

V7X_LANES = 128
V7X_SUBLANES = 8
RWKV_HEAD_DIM = 64
GMLP_BLOCK = 128
STREAM_CHUNK = 64
SCAN_CHUNK = 64
SCAN_HEADS = 4
VMEM_LIMIT = 60 * 1024 * 1024


def _cparams(*sem):
    return pltpu.CompilerParams(dimension_semantics=sem, vmem_limit_bytes=VMEM_LIMIT)


def _nt(a, b):
    return lax.dot_general(a, b, (((1,), (1,)), ((), ())), preferred_element_type=F32)


def _mm(a, b, precision=None):
    return jnp.dot(a, b, precision=precision, preferred_element_type=F32)


def _mmb(a, b):
    return _mm(a.astype(BF16), b.astype(BF16))


def _rms_kernel(x_ref, g_ref, o_ref):
    x = x_ref[...]
    ms = jnp.mean(x * x, axis=-1, keepdims=True)
    o_ref[...] = (x * lax.rsqrt(ms + RMS_EPS) * g_ref[...]).astype(o_ref.dtype)


def rms_norm_bf16(x, g, tm):
    n, d = x.shape
    return pl.pallas_call(
        _rms_kernel,
        grid=(n // tm,),
        in_specs=[pl.BlockSpec((tm, d), lambda i: (i, 0)), pl.BlockSpec((1, d), lambda i: (0, 0))],
        out_specs=pl.BlockSpec((tm, d), lambda i: (i, 0)),
        out_shape=jax.ShapeDtypeStruct((n, d), BF16),
        compiler_params=_cparams("parallel"),
        name="rms_norm",
    )(x, g.reshape(1, d))


def _mm_kernel(a_ref, w_ref, o_ref):
    o_ref[...] = _mm(a_ref[...], w_ref[...]).astype(o_ref.dtype)


def matmul(a, w, tm, tn, name):
    m, k = a.shape
    n = w.shape[1]
    return pl.pallas_call(
        _mm_kernel,
        grid=(m // tm, n // tn),
        in_specs=[pl.BlockSpec((tm, k), lambda i, j: (i, 0)), pl.BlockSpec((k, tn), lambda i, j: (0, j))],
        out_specs=pl.BlockSpec((tm, tn), lambda i, j: (i, j)),
        out_shape=jax.ShapeDtypeStruct((m, n), F32),
        compiler_params=_cparams("parallel", "arbitrary"),
        name=name,
    )(a, w)


def _mm2_kernel(a1_ref, a2_ref, w1_ref, w2_ref, o_ref):
    o_ref[...] = _mm(a1_ref[...], w1_ref[...]) + _mm(a2_ref[...], w2_ref[...])


def matmul_concat2(a1, a2, w, tm, tn, name):
    m, k1 = a1.shape
    k2 = a2.shape[1]
    n = w.shape[1]
    assert k1 == k2
    return pl.pallas_call(
        _mm2_kernel,
        grid=(m // tm, n // tn),
        in_specs=[pl.BlockSpec((tm, k1), lambda i, j: (i, 0)),
                  pl.BlockSpec((tm, k2), lambda i, j: (i, 0)),
                  pl.BlockSpec((k1, tn), lambda i, j: (0, j)),
                  pl.BlockSpec((k2, tn), lambda i, j: (1, j))],
        out_specs=pl.BlockSpec((tm, tn), lambda i, j: (i, j)),
        out_shape=jax.ShapeDtypeStruct((m, n), F32),
        compiler_params=_cparams("parallel", "arbitrary"),
        name=name,
    )(a1, a2, w, w)


def _gmlp_kernel(u_ref, v_ref, lng_ref, lnb_ref, ws_ref, bs_ref, o_ref, *, heads, nblk):
    zv = jax.nn.gelu(v_ref[...])
    mean = jnp.mean(zv, axis=-1, keepdims=True)
    xc = zv - mean
    var = jnp.mean(xc * xc, axis=-1, keepdims=True)
    vn = (xc * lax.rsqrt(var + LN_EPS) * lng_ref[...] + lnb_ref[...]).astype(BF16)
    row = lax.broadcasted_iota(jnp.int32, (GMLP_BLOCK, GMLP_BLOCK), 0) // STREAM_CHUNK
    col = lax.broadcasted_iota(jnp.int32, (GMLP_BLOCK, GMLP_BLOCK), 1) // STREAM_CHUNK
    causal = col <= row
    for h in range(heads):
        wm = jnp.where(causal, ws_ref[h], 0.0).astype(BF16)
        bias = bs_ref[h]
        cs = slice(h * V7X_LANES, (h + 1) * V7X_LANES)
        for n in range(nblk):
            rs = slice(n * GMLP_BLOCK, (n + 1) * GMLP_BLOCK)
            mixed = _mm(wm, vn[rs, cs]) + bias
            o_ref[rs, cs] = (jax.nn.gelu(u_ref[rs, cs]) * mixed).astype(o_ref.dtype)


def gmlp_gating(p_main, ln_g, ln_b, ws, bs, rows):
    n = p_main.shape[0]
    heads = ws.shape[0]
    gw = heads * V7X_LANES
    assert ws.shape[1:] == (GMLP_BLOCK, GMLP_BLOCK) and rows % GMLP_BLOCK == 0
    kern = functools.partial(_gmlp_kernel, heads=heads, nblk=rows // GMLP_BLOCK)
    return pl.pallas_call(
        kern,
        grid=(n // rows,),
        in_specs=[pl.BlockSpec((rows, gw), lambda i: (i, 0)),
                  pl.BlockSpec((rows, gw), lambda i: (i, 1)),
                  pl.BlockSpec((1, gw), lambda i: (0, 0)),
                  pl.BlockSpec((1, gw), lambda i: (0, 0)),
                  pl.BlockSpec((heads, GMLP_BLOCK, GMLP_BLOCK), lambda i: (0, 0, 0)),
                  pl.BlockSpec((heads, GMLP_BLOCK, 1), lambda i: (0, 0, 0))],
        out_specs=pl.BlockSpec((rows, gw), lambda i: (i, 0)),
        out_shape=jax.ShapeDtypeStruct((n, gw), BF16),
        compiler_params=_cparams("parallel"),
        name="gmlp_gating",
    )(p_main, p_main, ln_g.reshape(1, gw), ln_b.reshape(1, gw), ws, bs[:, :, None])


def _rwkv_kernel(r_ref, k_ref, v_ref, t_ref, rh_ref, kh_ref, vh_ref, th_ref,
                 mur_ref, muk_ref, muv_ref, mut_ref, w0_ref, wup_ref, a0_ref, aup_ref, gup_ref,
                 kk_ref, ka_ref, rk_ref, lng_ref, lnb_ref, o_ref, state_ref, *, heads, nchunks):
    C = SCAN_CHUNK
    N = RWKV_HEAD_DIM
    L = heads * N
    first = pl.program_id(2) == 0

    @pl.when(first)
    def _():
        state_ref[...] = jnp.zeros_like(state_ref)

    def shifted(ref, halo_ref, mu_ref):
        p = ref[...]
        halo = jnp.where(first, 0.0, halo_ref[V7X_SUBLANES - 1:V7X_SUBLANES, :])
        rows = lax.broadcasted_iota(jnp.int32, p.shape, 0)
        prev = jnp.where(rows == 0, halo, pltpu.roll(p, 1, 0))
        return p + (prev - p) * mu_ref[...]

    r_all = shifted(r_ref, rh_ref, mur_ref)
    k_all = shifted(k_ref, kh_ref, muk_ref)
    v_all = shifted(v_ref, vh_ref, muv_ref)
    tail = shifted(t_ref, th_ref, mut_ref)
    lora_in = tail[:, :V7X_LANES]
    xg = tail[:, V7X_LANES:]

    zw = w0_ref[...] + _mm(jnp.tanh(lora_in).astype(BF16), wup_ref[...])
    lw_all = jax.nn.sigmoid(zw) * (-DECAY_LOG_SCALE)
    a_all = jax.nn.sigmoid(a0_ref[...] + _mm(lora_in.astype(BF16), aup_ref[...]))
    g_all = _mm(jax.nn.sigmoid(xg).astype(BF16), gup_ref[...])

    r2 = lax.broadcasted_iota(jnp.int32, (L, L), 0)
    c2 = lax.broadcasted_iota(jnp.int32, (L, L), 1)
    bd_mask = (r2 // N) == (c2 // N)
    eye2 = r2 == c2
    ones_bd = bd_mask.astype(BF16)

    def head_sum(x):
        return _mm(x.astype(BF16), ones_bd)

    kk_all = k_all * kk_ref[...]
    kk_all = kk_all / jnp.maximum(jnp.sqrt(head_sum(kk_all * kk_all)), L2_EPS)
    k2_all = k_all * (1.0 + (a_all - 1.0) * ka_ref[...])
    bv_all = head_sum(r_all * k2_all * rk_ref[...]) * v_all
    beta_all = kk_all * a_all

    ti = lax.broadcasted_iota(jnp.int32, (C, C), 0)
    si = lax.broadcasted_iota(jnp.int32, (C, C), 1)
    tri = (ti >= si).astype(F32)
    lane = lax.broadcasted_iota(jnp.int32, (C, L), 1)
    rowi = lax.broadcasted_iota(jnp.int32, (C, L), 0)
    scol = lane % N
    strict = rowi > scol
    incl = rowi >= scol
    eye_p = (rowi == scol).astype(F32)
    head_sel = [lane // N == h for h in range(heads)]

    def sm(x):
        xb = x.astype(BF16)
        zero = jnp.zeros_like(xb)
        return jnp.concatenate([jnp.where(sel, xb, zero) for sel in head_sel], axis=0)

    chunks = range(nchunks)

    def each(fn, *lists):
        return [fn(*xs) for xs in zip(*lists)]

    rows = [slice(c * C, (c + 1) * C) for c in chunks]
    lw = [lw_all[rs] for rs in rows]
    cm = [_mm(tri, x, HIGHEST) for x in lw]
    r = [r_all[rs] for rs in rows]
    k = [k2_all[rs] for rs in rows]
    v = [v_all[rs] for rs in rows]
    kk = [kk_all[rs] for rs in rows]
    beta = [beta_all[rs] for rs in rows]
    ginv = each(lambda m: jnp.exp(-m), cm)
    a_t = each(lambda kk_, m, w: -kk_ * jnp.exp(m - w), kk, cm, lw)
    b_t = each(jnp.multiply, beta, ginv)
    k_t = each(jnp.multiply, k, ginv)
    r_t = each(lambda r_, m: r_ * jnp.exp(m), r, cm)
    dec_end = each(lambda m: jnp.exp(jnp.broadcast_to(m[C - 1:C, :], (C, L)) - m), cm)
    b_h = each(jnp.multiply, beta, dec_end)
    k_h = each(jnp.multiply, k, dec_end)

    sc = each(lambda a_, r_, b_, k_: _nt(jnp.concatenate([a_, r_], axis=0).astype(BF16),
                                         jnp.concatenate([sm(b_), sm(k_)], axis=0)),
              a_t, r_t, b_t, k_t)
    ab = [x[:C, :L] for x in sc]
    l_ak = [jnp.where(strict, x[:C, L:], 0.0) for x in sc]
    p_rb = [jnp.where(incl, x[C:, :L], 0.0) for x in sc]
    p_rk = [jnp.where(incl, x[C:, L:], 0.0) for x in sc]

    base = 8
    d = [jnp.where(strict & (rowi // base == scol // base), x, 0.0) for x in ab]
    t_inv = [eye_p + x for x in d]
    p2 = each(lambda x: _mmb(x, sm(x)), d)
    both = each(lambda p, t: _mmb(p, jnp.concatenate([sm(p), sm(t)], axis=1)), p2, t_inv)
    t_inv = each(lambda t, bo: t + bo[:, L:], t_inv, both)
    t_inv = each(lambda t, bo: t + _mmb(bo[:, :L], sm(t)), t_inv, both)
    b = base
    while b < C:
        blk_off = (rowi // (2 * b) == scol // (2 * b)) & (rowi // b > scol // b)
        ot = each(lambda x, t: _mmb(jnp.where(blk_off, x, 0.0), sm(t)), ab, t_inv)
        t_inv = each(lambda t, o: t + _mmb(t, sm(o)), t_inv, ot)
        b *= 2

    sm_v = each(sm, v)
    x1 = each(_mmb, l_ak, sm_v)
    wv = each(lambda t, a_, x_: _mmb(t, jnp.concatenate([sm(a_), sm(x_)], axis=1)), t_inv, a_t, x1)
    w_t = [x[:, :L] for x in wv]
    v_t = [x[:, L:] for x in wv]
    q = each(lambda r_, p, w: (r_ + _mmb(p, sm(w))).astype(BF16), r_t, p_rb, w_t)
    y0 = each(lambda pb, pk, vt, sv: _mmb(jnp.concatenate([pb, pk], axis=1), jnp.concatenate([sm(vt), sv], axis=0)),
              p_rb, p_rk, v_t, sm_v)
    b_ht = [x.T for x in b_h]
    k_ht = [x.T for x in k_h]
    m_bd = each(lambda bt, w: jnp.where(bd_mask, _mmb(bt, w), 0.0).astype(BF16), b_ht, w_t)
    n_bd = each(lambda bt, kt, vt, v_: jnp.where(bd_mask, _mmb(jnp.concatenate([bt, kt], axis=1),
                                                                jnp.concatenate([vt, v_], axis=0)), 0.0),
                b_ht, k_ht, v_t, v)
    gcol = each(lambda m: jnp.sum(jnp.where(eye2, jnp.exp(jnp.broadcast_to(m[C - 1:C, :], (L, L))), 0.0),
                                  axis=1, keepdims=True), cm)

    S = state_ref[...]
    ys = []
    for c in chunks:
        sb = S.astype(BF16)
        ys.append(_mm(q[c], sb) + y0[c])
        S = gcol[c] * S + _mm(m_bd[c], sb) + n_bd[c]
    state_ref[...] = S

    y = jnp.concatenate(ys, axis=0)
    inv_n = 1.0 / N
    yc = y - head_sum(y) * inv_n
    var = head_sum(yc * yc) * inv_n
    yn = yc * lax.rsqrt(var + N * GN_EPS_PER_CH) * lng_ref[...] + lnb_ref[...]
    o_ref[...] = ((yn + bv_all) * g_all).astype(o_ref.dtype)


def rwkv_time_mix(p_main, p_tail, mu, w0, w_up, a0, a_up, g_up, k_k, k_a, r_k, lnx_g, lnx_b,
                  batch, seq, tc, first_col):
    n = p_main.shape[0]
    rw = w0.shape[0]
    L = SCAN_HEADS * RWKV_HEAD_DIM
    dl, il = w_up.shape[0], a_up.shape[0]
    tw = p_tail.shape[1]
    assert dl + il == V7X_LANES and seq % tc == 0 and tc % SCAN_CHUNK == 0 and rw % L == 0 and first_col % L == 0
    nt = seq // tc
    ng = rw // L
    cb = first_col // L
    sub = V7X_SUBLANES
    row = lambda a: a.reshape(1, -1)
    mur, muk, muv, mut = mu[:rw], mu[rw:2 * rw], mu[2 * rw:3 * rw], mu[3 * rw:]
    wup_pad = jnp.concatenate([w_up, jnp.zeros((il, rw), w_up.dtype)], axis=0).astype(BF16)
    aup_pad = jnp.concatenate([jnp.zeros((dl, rw), a_up.dtype), a_up], axis=0).astype(BF16)
    gup = g_up.astype(BF16)

    def halo_row(b, c):
        return jnp.maximum((b * nt + c) * (tc // sub) - 1, 0)

    blk = lambda i: pl.BlockSpec((tc, L), lambda b, g, c: (b * nt + c, cb + i * ng + g))
    halo = lambda i: pl.BlockSpec((sub, L), lambda b, g, c: (halo_row(b, c), cb + i * ng + g))
    vec = pl.BlockSpec((1, L), lambda b, g, c: (0, g))
    lora = lambda a: pl.BlockSpec((a.shape[0], L), lambda b, g, c: (0, g))
    return pl.pallas_call(
        functools.partial(_rwkv_kernel, heads=SCAN_HEADS, nchunks=tc // SCAN_CHUNK),
        grid=(batch, ng, nt),
        in_specs=[blk(0), blk(1), blk(2), pl.BlockSpec((tc, tw), lambda b, g, c: (b * nt + c, 0)),
                  halo(0), halo(1), halo(2), pl.BlockSpec((sub, tw), lambda b, g, c: (halo_row(b, c), 0)),
                  vec, vec, vec, pl.BlockSpec((1, tw), lambda b, g, c: (0, 0)),
                  vec, lora(wup_pad), vec, lora(aup_pad), lora(gup), vec, vec, vec, vec, vec],
        out_specs=pl.BlockSpec((tc, L), lambda b, g, c: (b * nt + c, g)),
        out_shape=jax.ShapeDtypeStruct((n, rw), BF16),
        scratch_shapes=[pltpu.VMEM((L, L), F32)],
        compiler_params=_cparams("parallel", "parallel", "arbitrary"),
        name="rwkv_time_mix",
    )(p_main, p_main, p_main, p_tail, p_main, p_main, p_main, p_tail,
      row(mur), row(muk), row(muv), row(mut), row(w0), wup_pad, row(a0), aup_pad, gup,
      row(k_k), row(k_a), row(r_k), row(lnx_g), row(lnx_b))


def _post_mix_kernel(mix_ref, x_ref, gpost_ref, gpre_ref, x1_ref, xn_ref):
    m = mix_ref[...]
    x1 = x_ref[...] + m * lax.rsqrt(jnp.mean(m * m, axis=-1, keepdims=True) + RMS_EPS) * gpost_ref[...]
    x1_ref[...] = x1
    xn_ref[...] = (x1 * lax.rsqrt(jnp.mean(x1 * x1, axis=-1, keepdims=True) + RMS_EPS)
                   * gpre_ref[...]).astype(xn_ref.dtype)


def post_mix(mix, x, g_post, g_pre, tm):
    n, d = x.shape
    blk = pl.BlockSpec((tm, d), lambda i: (i, 0))
    vec = pl.BlockSpec((1, d), lambda i: (0, 0))
    return pl.pallas_call(
        _post_mix_kernel,
        grid=(n // tm,),
        in_specs=[blk, blk, vec, vec],
        out_specs=[blk, blk],
        out_shape=[jax.ShapeDtypeStruct((n, d), F32), jax.ShapeDtypeStruct((n, d), BF16)],
        compiler_params=_cparams("parallel"),
        name="post_mix",
    )(mix, x, g_post.reshape(1, d), g_pre.reshape(1, d))


def _ffn_kernel(xn_ref, w1_ref, w2_ref, o_ref):
    @pl.when(pl.program_id(1) == 0)
    def _():
        o_ref[...] = jnp.zeros_like(o_ref)

    h = jnp.maximum(_mm(xn_ref[...], w1_ref[...]), 0.0)
    o_ref[...] += _mm((h * h).astype(BF16), w2_ref[...])


def ffn(xn, w1, w2, tm, tf):
    n, d = xn.shape
    f = w1.shape[1]
    return pl.pallas_call(
        _ffn_kernel,
        grid=(n // tm, f // tf),
        in_specs=[pl.BlockSpec((tm, d), lambda i, j: (i, 0), pipeline_mode=pl.Buffered(1)),
                  pl.BlockSpec((d, tf), lambda i, j: (0, j)),
                  pl.BlockSpec((tf, d), lambda i, j: (j, 0))],
        out_specs=pl.BlockSpec((tm, d), lambda i, j: (i, 0)),
        out_shape=jax.ShapeDtypeStruct((n, d), F32),
        compiler_params=_cparams("parallel", "arbitrary"),
        name="ffn",
    )(xn, w1, w2)


def _post_ffn_kernel(f_ref, x_ref, g_ref, o_ref):
    f = f_ref[...]
    o_ref[...] = x_ref[...] + f * lax.rsqrt(jnp.mean(f * f, axis=-1, keepdims=True) + RMS_EPS) * g_ref[...]


def post_ffn(f, x1, g, tm):
    n, d = x1.shape
    blk = pl.BlockSpec((tm, d), lambda i: (i, 0))
    return pl.pallas_call(
        _post_ffn_kernel,
        grid=(n // tm,),
        in_specs=[blk, blk, pl.BlockSpec((1, d), lambda i: (0, 0))],
        out_specs=blk,
        out_shape=jax.ShapeDtypeStruct((n, d), F32),
        compiler_params=_cparams("parallel"),
        name="post_ffn",
    )(f, x1, g.reshape(1, d))


def _tile(n, want):
    if n <= want:
        return n
    t = want - want % V7X_LANES
    while n % t:
        t -= V7X_LANES
    assert t > 0, (n, want)
    return t


def kernel(x, pre_mix_g, w_in, tshift_mu, gmlp_ln_g, gmlp_ln_b, gmlp_ws, gmlp_bs, decay_w0, decay_up, iclr_a0, iclr_up, gate_up, k_k, k_a, r_k, lnx_g, lnx_b, w_out, post_mix_g, pre_ffn_g, w_ff1, w_ff2, post_ffn_g):
    batch, seq, d = x.shape
    depth = w_in.shape[0]
    n = batch * seq
    gw = gmlp_ws.shape[1] * V7X_LANES
    rw = decay_w0.shape[1]
    main_w = 2 * gw + 3 * rw
    xf = x.reshape(n, d)
    for l in range(depth):
        w_in_main = w_in[l, :, :main_w].astype(BF16)
        w_in_tail = w_in[l, :, main_w:].astype(BF16)
        h = rms_norm_bf16(xf, pre_mix_g[l], _tile(n, 512))
        p_main = matmul(h, w_in_main, _tile(n, 1024), _tile(main_w, 1024), "w_in_main")
        p_tail = matmul(h, w_in_tail, _tile(n, 1024), w_in_tail.shape[1], "w_in_tail")

        y_a = gmlp_gating(p_main, gmlp_ln_g[l], gmlp_ln_b[l], gmlp_ws[l], gmlp_bs[l], _tile(seq, 512))
        y_b = rwkv_time_mix(p_main, p_tail, tshift_mu[l], decay_w0[l], decay_up[l], iclr_a0[l], iclr_up[l],
                            gate_up[l], k_k[l], k_a[l], r_k[l].reshape(-1), lnx_g[l], lnx_b[l],
                            batch, seq, _tile(seq, 512), first_col=2 * gw)

        mix = matmul_concat2(y_a, y_b, w_out[l].astype(BF16), _tile(n, 1024), _tile(d, 1024), "w_out")
        x1, xn = post_mix(mix, xf, post_mix_g[l], pre_ffn_g[l], _tile(n, 256))
        f = ffn(xn, w_ff1[l].astype(BF16), w_ff2[l].astype(BF16), _tile(n, 512), _tile(w_ff1.shape[2], 1024))
        xf = post_ffn(f, x1, post_ffn_g[l], _tile(n, 256))
    return xf.reshape(batch, seq, d)
```

```python
import functools
import math

import jax
import jax.numpy as jnp
from jax import lax
from jax.experimental import pallas as pl
from jax.experimental.pallas import tpu as pltpu

F32 = jnp.float32
BF16 = jnp.bfloat16
HIGHEST = lax.Precision.HIGHEST

RMS_EPS = 1e-6
LN_EPS = 1e-5
L2_EPS = 1e-12
GN_EPS_PER_CH = 1e-5
DECAY_LOG_SCALE = math.exp(-0.5)

V7X_LANES = 128
V7X_SUBLANES = 8
RWKV_HEAD_DIM = 64
GMLP_BLOCK = 128
STREAM_CHUNK = 64
SCAN_CHUNK = 64
SCAN_HEADS = 4
VMEM_LIMIT = 60 * 1024 * 1024


def _cparams(*sem):
    return pltpu.CompilerParams(dimension_semantics=sem, vmem_limit_bytes=VMEM_LIMIT)


def _nt(a, b):
    return lax.dot_general(a, b, (((1,), (1,)), ((), ())), preferred_element_type=F32)


def _mm(a, b, precision=None):
    return jnp.dot(a, b, precision=precision, preferred_element_type=F32)


def _mmb(a, b):
    return _mm(a.astype(BF16), b.astype(BF16))


def _rms_kernel(x_ref, g_ref, o_ref):
    x = x_ref[...]
    ms = jnp.mean(x * x, axis=-1, keepdims=True)
    o_ref[...] = (x * lax.rsqrt(ms + RMS_EPS) * g_ref[...]).astype(o_ref.dtype)


def rms_norm_bf16(x, g, tm):
    n, d = x.shape
    return pl.pallas_call(
        _rms_kernel,
        grid=(n // tm,),
        in_specs=[pl.BlockSpec((tm, d), lambda i: (i, 0)), pl.BlockSpec((1, d), lambda i: (0, 0))],
        out_specs=pl.BlockSpec((tm, d), lambda i: (i, 0)),
        out_shape=jax.ShapeDtypeStruct((n, d), BF16),
        compiler_params=_cparams("parallel"),
        name="rms_norm",
    )(x, g.reshape(1, d))


def _mm_kernel(a_ref, w_ref, o_ref):
    o_ref[...] = _mm(a_ref[...], w_ref[...].astype(a_ref.dtype)).astype(o_ref.dtype)


def matmul(a, w, tm, tn, name):
    m, k = a.shape
    n = w.shape[1]
    return pl.pallas_call(
        _mm_kernel,
        grid=(m // tm, pl.cdiv(n, tn)),
        in_specs=[pl.BlockSpec((tm, k), lambda i, j: (i, 0)), pl.BlockSpec((k, tn), lambda i, j: (0, j))],
        out_specs=pl.BlockSpec((tm, tn), lambda i, j: (i, j)),
        out_shape=jax.ShapeDtypeStruct((m, n), F32),
        compiler_params=_cparams("parallel", "arbitrary"),
        name=name,
    )(a, w)


def _mm2_kernel(a1_ref, a2_ref, w1_ref, w2_ref, o_ref):
    o_ref[...] = _mm(a1_ref[...], w1_ref[...]) + _mm(a2_ref[...], w2_ref[...])


def matmul_concat2(a1, a2, w, tm, tn, name):
    m, k1 = a1.shape
    k2 = a2.shape[1]
    n = w.shape[1]
    assert k1 == k2
    return pl.pallas_call(
        _mm2_kernel,
        grid=(m // tm, n // tn),
        in_specs=[pl.BlockSpec((tm, k1), lambda i, j: (i, 0)),
                  pl.BlockSpec((tm, k2), lambda i, j: (i, 0)),
                  pl.BlockSpec((k1, tn), lambda i, j: (0, j)),
                  pl.BlockSpec((k2, tn), lambda i, j: (1, j))],
        out_specs=pl.BlockSpec((tm, tn), lambda i, j: (i, j)),
        out_shape=jax.ShapeDtypeStruct((m, n), F32),
        compiler_params=_cparams("parallel", "arbitrary"),
        name=name,
    )(a1, a2, w, w)


def _gmlp_kernel(u_ref, v_ref, lng_ref, lnb_ref, ws_ref, bs_ref, o_ref, *, heads, nblk):
    zv = jax.nn.gelu(v_ref[...])
    mean = jnp.mean(zv, axis=-1, keepdims=True)
    xc = zv - mean
    var = jnp.mean(xc * xc, axis=-1, keepdims=True)
    vn = (xc * lax.rsqrt(var + LN_EPS) * lng_ref[...] + lnb_ref[...]).astype(BF16)
    row = lax.broadcasted_iota(jnp.int32, (GMLP_BLOCK, GMLP_BLOCK), 0) // STREAM_CHUNK
    col = lax.broadcasted_iota(jnp.int32, (GMLP_BLOCK, GMLP_BLOCK), 1) // STREAM_CHUNK
    causal = col <= row
    for h in range(heads):
        wm = jnp.where(causal, ws_ref[h], 0.0).astype(BF16)
        bias = bs_ref[h]
        cs = slice(h * V7X_LANES, (h + 1) * V7X_LANES)
        for n in range(nblk):
            rs = slice(n * GMLP_BLOCK, (n + 1) * GMLP_BLOCK)
            mixed = _mm(wm, vn[rs, cs]) + bias
            o_ref[rs, cs] = (jax.nn.gelu(u_ref[rs, cs]) * mixed).astype(o_ref.dtype)


def gmlp_gating(p_main, ln_g, ln_b, ws, bs, rows):
    n = p_main.shape[0]
    heads = ws.shape[0]
    gw = heads * V7X_LANES
    assert ws.shape[1:] == (GMLP_BLOCK, GMLP_BLOCK) and rows % GMLP_BLOCK == 0
    kern = functools.partial(_gmlp_kernel, heads=heads, nblk=rows // GMLP_BLOCK)
    return pl.pallas_call(
        kern,
        grid=(n // rows,),
        in_specs=[pl.BlockSpec((rows, gw), lambda i: (i, 0)),
                  pl.BlockSpec((rows, gw), lambda i: (i, 1)),
                  pl.BlockSpec((1, gw), lambda i: (0, 0)),
                  pl.BlockSpec((1, gw), lambda i: (0, 0)),
                  pl.BlockSpec((heads, GMLP_BLOCK, GMLP_BLOCK), lambda i: (0, 0, 0)),
                  pl.BlockSpec((heads, GMLP_BLOCK, 1), lambda i: (0, 0, 0))],
        out_specs=pl.BlockSpec((rows, gw), lambda i: (i, 0)),
        out_shape=jax.ShapeDtypeStruct((n, gw), BF16),
        compiler_params=_cparams("parallel"),
        name="gmlp_gating",
    )(p_main, p_main, ln_g.reshape(1, gw), ln_b.reshape(1, gw), ws, bs[:, :, None])


def _rwkv_kernel(r_ref, k_ref, v_ref, t0_ref, t1_ref, t2_ref, rh_ref, kh_ref, vh_ref, th0_ref, th1_ref, th2_ref,
                 mur_ref, muk_ref, muv_ref, mut_ref, w0_ref, wup_ref, a0_ref, aup_ref, gup_ref,
                 kk_ref, ka_ref, rk_ref, lng_ref, lnb_ref, o_ref, state_ref, *, heads, nchunks):
    C = SCAN_CHUNK
    N = RWKV_HEAD_DIM
    L = heads * N
    first = pl.program_id(2) == 0

    @pl.when(first)
    def _():
        state_ref[...] = jnp.zeros_like(state_ref)

    def shifted(ref, halo_ref, mu):
        p = ref[...]
        halo = jnp.where(first, 0.0, halo_ref[V7X_SUBLANES - 1:V7X_SUBLANES, :])
        rows = lax.broadcasted_iota(jnp.int32, p.shape, 0)
        prev = jnp.where(rows == 0, halo, pltpu.roll(p, 1, 0))
        return p + (prev - p) * mu

    r_all = shifted(r_ref, rh_ref, mur_ref[...])
    k_all = shifted(k_ref, kh_ref, muk_ref[...])
    v_all = shifted(v_ref, vh_ref, muv_ref[...])
    mut = mut_ref[...]
    W = V7X_LANES
    lora_in = shifted(t0_ref, th0_ref, mut[:, :W])
    xg = jnp.concatenate([shifted(t1_ref, th1_ref, mut[:, W:2 * W]),
                          shifted(t2_ref, th2_ref, mut[:, 2 * W:])], axis=1)

    zw = w0_ref[...] + _mm(jnp.tanh(lora_in).astype(BF16), wup_ref[...])
    lw_all = jax.nn.sigmoid(zw) * (-DECAY_LOG_SCALE)
    a_all = jax.nn.sigmoid(a0_ref[...] + _mm(lora_in.astype(BF16), aup_ref[...]))
    g_all = _mm(jax.nn.sigmoid(xg).astype(BF16), gup_ref[...])

    r2 = lax.broadcasted_iota(jnp.int32, (L, L), 0)
    c2 = lax.broadcasted_iota(jnp.int32, (L, L), 1)
    bd_mask = (r2 // N) == (c2 // N)
    eye2 = r2 == c2
    ones_bd = bd_mask.astype(BF16)

    def head_sum(x):
        return _mm(x.astype(BF16), ones_bd)

    kk_all = k_all * kk_ref[...]
    kk_all = kk_all / jnp.maximum(jnp.sqrt(head_sum(kk_all * kk_all)), L2_EPS)
    k2_all = k_all * (1.0 + (a_all - 1.0) * ka_ref[...])
    bv_all = head_sum(r_all * k2_all * rk_ref[...]) * v_all
    beta_all = kk_all * a_all

    ti = lax.broadcasted_iota(jnp.int32, (C, C), 0)
    si = lax.broadcasted_iota(jnp.int32, (C, C), 1)
    tri = (ti >= si).astype(F32)
    lane = lax.broadcasted_iota(jnp.int32, (C, L), 1)
    rowi = lax.broadcasted_iota(jnp.int32, (C, L), 0)
    scol = lane % N
    strict = rowi > scol
    incl = rowi >= scol
    eye_p = (rowi == scol).astype(F32)
    head_sel = [lane // N == h for h in range(heads)]

    def sm(x):
        xb = x.astype(BF16)
        zero = jnp.zeros_like(xb)
        return jnp.concatenate([jnp.where(sel, xb, zero) for sel in head_sel], axis=0)

    chunks = range(nchunks)

    def each(fn, *lists):
        return [fn(*xs) for xs in zip(*lists)]

    rows = [slice(c * C, (c + 1) * C) for c in chunks]
    lw = [lw_all[rs] for rs in rows]
    cm = [_mm(tri, x, HIGHEST) for x in lw]
    r = [r_all[rs] for rs in rows]
    k = [k2_all[rs] for rs in rows]
    v = [v_all[rs] for rs in rows]
    kk = [kk_all[rs] for rs in rows]
    beta = [beta_all[rs] for rs in rows]
    ginv = each(lambda m: jnp.exp(-m), cm)
    a_t = each(lambda kk_, m, w: -kk_ * jnp.exp(m - w), kk, cm, lw)
    b_t = each(jnp.multiply, beta, ginv)
    k_t = each(jnp.multiply, k, ginv)
    r_t = each(lambda r_, m: r_ * jnp.exp(m), r, cm)
    dec_end = each(lambda m: jnp.exp(jnp.broadcast_to(m[C - 1:C, :], (C, L)) - m), cm)
    b_h = each(jnp.multiply, beta, dec_end)
    k_h = each(jnp.multiply, k, dec_end)

    sc = each(lambda a_, r_, b_, k_: _nt(jnp.concatenate([a_, r_], axis=0).astype(BF16),
                                         jnp.concatenate([sm(b_), sm(k_)], axis=0)),
              a_t, r_t, b_t, k_t)
    ab = [x[:C, :L] for x in sc]
    l_ak = [jnp.where(strict, x[:C, L:], 0.0) for x in sc]
    p_rb = [jnp.where(incl, x[C:, :L], 0.0) for x in sc]
    p_rk = [jnp.where(incl, x[C:, L:], 0.0) for x in sc]

    base = 8
    d = [jnp.where(strict & (rowi // base == scol // base), x, 0.0) for x in ab]
    t_inv = [eye_p + x for x in d]
    p2 = each(lambda x: _mmb(x, sm(x)), d)
    both = each(lambda p, t: _mmb(p, jnp.concatenate([sm(p), sm(t)], axis=1)), p2, t_inv)
    t_inv = each(lambda t, bo: t + bo[:, L:], t_inv, both)
    t_inv = each(lambda t, bo: t + _mmb(bo[:, :L], sm(t)), t_inv, both)
    b = base
    while b < C:
        blk_off = (rowi // (2 * b) == scol // (2 * b)) & (rowi // b > scol // b)
        ot = each(lambda x, t: _mmb(jnp.where(blk_off, x, 0.0), sm(t)), ab, t_inv)
        t_inv = each(lambda t, o: t + _mmb(t, sm(o)), t_inv, ot)
        b *= 2

    sm_v = each(sm, v)
    x1 = each(_mmb, l_ak, sm_v)
    wv = each(lambda t, a_, x_: _mmb(t, jnp.concatenate([sm(a_), sm(x_)], axis=1)), t_inv, a_t, x1)
    w_t = [x[:, :L] for x in wv]
    v_t = [x[:, L:] for x in wv]
    q = each(lambda r_, p, w: (r_ + _mmb(p, sm(w))).astype(BF16), r_t, p_rb, w_t)
    y0 = each(lambda pb, pk, vt, sv: _mmb(jnp.concatenate([pb, pk], axis=1), jnp.concatenate([sm(vt), sv], axis=0)),
              p_rb, p_rk, v_t, sm_v)
    b_ht = [x.T for x in b_h]
    k_ht = [x.T for x in k_h]
    m_bd = each(lambda bt, w: jnp.where(bd_mask, _mmb(bt, w), 0.0).astype(BF16), b_ht, w_t)
    n_bd = each(lambda bt, kt, vt, v_: jnp.where(bd_mask, _mmb(jnp.concatenate([bt, kt], axis=1),
                                                                jnp.concatenate([vt, v_], axis=0)), 0.0),
                b_ht, k_ht, v_t, v)
    gcol = each(lambda m: jnp.sum(jnp.where(eye2, jnp.exp(jnp.broadcast_to(m[C - 1:C, :], (L, L))), 0.0),
                                  axis=1, keepdims=True), cm)

    S = state_ref[...]
    ys = []
    for c in chunks:
        sb = S.astype(BF16)
        ys.append(_mm(q[c], sb) + y0[c])
        S = gcol[c] * S + _mm(m_bd[c], sb) + n_bd[c]
    state_ref[...] = S

    y = jnp.concatenate(ys, axis=0)
    inv_n = 1.0 / N
    yc = y - head_sum(y) * inv_n
    var = head_sum(yc * yc) * inv_n
    yn = yc * lax.rsqrt(var + N * GN_EPS_PER_CH) * lng_ref[...] + lnb_ref[...]
    o_ref[...] = ((yn + bv_all) * g_all).astype(o_ref.dtype)


def rwkv_time_mix(p, mu, w0, w_up, a0, a_up, g_up, k_k, k_a, r_k, lnx_g, lnx_b,
                  batch, seq, tc, first_col):
    n = p.shape[0]
    rw = w0.shape[0]
    L = SCAN_HEADS * RWKV_HEAD_DIM
    dl, il = w_up.shape[0], a_up.shape[0]
    W = V7X_LANES
    tw = p.shape[1] - first_col - 3 * rw
    tb = (first_col + 3 * rw) // W
    assert tw == 3 * W and g_up.shape[0] == 2 * W and (first_col + 3 * rw) % W == 0
    assert dl + il == V7X_LANES and seq % tc == 0 and tc % SCAN_CHUNK == 0 and rw % L == 0 and first_col % L == 0
    nt = seq // tc
    ng = rw // L
    cb = first_col // L
    sub = V7X_SUBLANES
    row = lambda a: a.reshape(1, -1)
    mur, muk, muv, mut = mu[:rw], mu[rw:2 * rw], mu[2 * rw:3 * rw], mu[3 * rw:]
    wup_pad = jnp.concatenate([w_up, jnp.zeros((il, rw), w_up.dtype)], axis=0).astype(BF16)
    aup_pad = jnp.concatenate([jnp.zeros((dl, rw), a_up.dtype), a_up], axis=0).astype(BF16)
    gup = g_up.astype(BF16)

    def halo_row(b, c):
        return jnp.maximum((b * nt + c) * (tc // sub) - 1, 0)

    blk = lambda i: pl.BlockSpec((tc, L), lambda b, g, c: (b * nt + c, cb + i * ng + g))
    halo = lambda i: pl.BlockSpec((sub, L), lambda b, g, c: (halo_row(b, c), cb + i * ng + g))
    tail = lambda j: pl.BlockSpec((tc, W), lambda b, g, c: (b * nt + c, tb + j))
    tail_halo = lambda j: pl.BlockSpec((sub, W), lambda b, g, c: (halo_row(b, c), tb + j))
    vec = pl.BlockSpec((1, L), lambda b, g, c: (0, g))
    lora = lambda a: pl.BlockSpec((a.shape[0], L), lambda b, g, c: (0, g))
    return pl.pallas_call(
        functools.partial(_rwkv_kernel, heads=SCAN_HEADS, nchunks=tc // SCAN_CHUNK),
        grid=(batch, ng, nt),
        in_specs=[blk(0), blk(1), blk(2), tail(0), tail(1), tail(2),
                  halo(0), halo(1), halo(2), tail_halo(0), tail_halo(1), tail_halo(2),
                  vec, vec, vec, pl.BlockSpec((1, tw), lambda b, g, c: (0, 0)),
                  vec, lora(wup_pad), vec, lora(aup_pad), lora(gup), vec, vec, vec, vec, vec],
        out_specs=pl.BlockSpec((tc, L), lambda b, g, c: (b * nt + c, g)),
        out_shape=jax.ShapeDtypeStruct((n, rw), BF16),
        scratch_shapes=[pltpu.VMEM((L, L), F32)],
        compiler_params=_cparams("parallel", "parallel", "arbitrary"),
        name="rwkv_time_mix",
    )(*([p] * 12),
      row(mur), row(muk), row(muv), row(mut), row(w0), wup_pad, row(a0), aup_pad, gup,
      row(k_k), row(k_a), row(r_k), row(lnx_g), row(lnx_b))


def _post_mix_kernel(mix_ref, x_ref, gpost_ref, gpre_ref, x1_ref, xn_ref):
    m = mix_ref[...]
    x1 = x_ref[...] + m * lax.rsqrt(jnp.mean(m * m, axis=-1, keepdims=True) + RMS_EPS) * gpost_ref[...]
    x1_ref[...] = x1
    xn_ref[...] = (x1 * lax.rsqrt(jnp.mean(x1 * x1, axis=-1, keepdims=True) + RMS_EPS)
                   * gpre_ref[...]).astype(xn_ref.dtype)


def post_mix(mix, x, g_post, g_pre, tm):
    n, d = x.shape
    blk = pl.BlockSpec((tm, d), lambda i: (i, 0))
    vec = pl.BlockSpec((1, d), lambda i: (0, 0))
    return pl.pallas_call(
        _post_mix_kernel,
        grid=(n // tm,),
        in_specs=[blk, blk, vec, vec],
        out_specs=[blk, blk],
        out_shape=[jax.ShapeDtypeStruct((n, d), F32), jax.ShapeDtypeStruct((n, d), BF16)],
        compiler_params=_cparams("parallel"),
        name="post_mix",
    )(mix, x, g_post.reshape(1, d), g_pre.reshape(1, d))


def _ffn_kernel(xn_ref, w1_ref, w2_ref, o_ref):
    @pl.when(pl.program_id(1) == 0)
    def _():
        o_ref[...] = jnp.zeros_like(o_ref)

    h = jnp.maximum(_mm(xn_ref[...], w1_ref[...]), 0.0)
    o_ref[...] += _mm((h * h).astype(BF16), w2_ref[...])


def ffn(xn, w1, w2, tm, tf):
    n, d = xn.shape
    f = w1.shape[1]
    return pl.pallas_call(
        _ffn_kernel,
        grid=(n // tm, f // tf),
        in_specs=[pl.BlockSpec((tm, d), lambda i, j: (i, 0), pipeline_mode=pl.Buffered(1)),
                  pl.BlockSpec((d, tf), lambda i, j: (0, j)),
                  pl.BlockSpec((tf, d), lambda i, j: (j, 0))],
        out_specs=pl.BlockSpec((tm, d), lambda i, j: (i, 0)),
        out_shape=jax.ShapeDtypeStruct((n, d), F32),
        compiler_params=_cparams("parallel", "arbitrary"),
        name="ffn",
    )(xn, w1, w2)


def _post_ffn_kernel(f_ref, x_ref, g_ref, o_ref):
    f = f_ref[...]
    o_ref[...] = x_ref[...] + f * lax.rsqrt(jnp.mean(f * f, axis=-1, keepdims=True) + RMS_EPS) * g_ref[...]


def post_ffn(f, x1, g, tm):
    n, d = x1.shape
    blk = pl.BlockSpec((tm, d), lambda i: (i, 0))
    return pl.pallas_call(
        _post_ffn_kernel,
        grid=(n // tm,),
        in_specs=[blk, blk, pl.BlockSpec((1, d), lambda i: (0, 0))],
        out_specs=blk,
        out_shape=jax.ShapeDtypeStruct((n, d), F32),
        compiler_params=_cparams("parallel"),
        name="post_ffn",
    )(f, x1, g.reshape(1, d))


def _tile(n, want):
    if n <= want:
        return n
    t = want - want % V7X_LANES
    while n % t:
        t -= V7X_LANES
    assert t > 0, (n, want)
    return t


def kernel(x, pre_mix_g, w_in, tshift_mu, gmlp_ln_g, gmlp_ln_b, gmlp_ws, gmlp_bs, decay_w0, decay_up, iclr_a0, iclr_up, gate_up, k_k, k_a, r_k, lnx_g, lnx_b, w_out, post_mix_g, pre_ffn_g, w_ff1, w_ff2, post_ffn_g):
    batch, seq, d = x.shape
    depth = w_in.shape[0]
    n = batch * seq
    gw = gmlp_ws.shape[1] * V7X_LANES
    rw = decay_w0.shape[1]
    xf = x.reshape(n, d)
    for l in range(depth):
        h = rms_norm_bf16(xf, pre_mix_g[l], _tile(n, 512))
        p = matmul(h, w_in[l], _tile(n, 1024), 512, "w_in")

        y_a = gmlp_gating(p, gmlp_ln_g[l], gmlp_ln_b[l], gmlp_ws[l], gmlp_bs[l], _tile(seq, 512))
        y_b = rwkv_time_mix(p, tshift_mu[l], decay_w0[l], decay_up[l], iclr_a0[l], iclr_up[l],
                            gate_up[l], k_k[l], k_a[l], r_k[l].reshape(-1), lnx_g[l], lnx_b[l],
                            batch, seq, _tile(seq, 512), first_col=2 * gw)

        mix = matmul_concat2(y_a, y_b, w_out[l].astype(BF16), _tile(n, 1024), _tile(d, 1024), "w_out")
        x1, xn = post_mix(mix, xf, post_mix_g[l], pre_ffn_g[l], _tile(n, 256))
        f = ffn(xn, w_ff1[l].astype(BF16), w_ff2[l].astype(BF16), _tile(n, 512), _tile(w_ff1.shape[2], 1024))
        xf = post_ffn(f, x1, post_ffn_g[l], _tile(n, 256))
    return xf.reshape(batch, seq, d)
```

```python
import functools
import math

import jax
import jax.numpy as jnp
from jax import lax
from jax.experimental import pallas as pl
from jax.experimental.pallas import tpu as pltpu

F32 = jnp.float32
BF16 = jnp.bfloat16
HIGHEST = lax.Precision.HIGHEST

RMS_EPS = 1e-6
LN_EPS = 1e-5
L2_EPS = 1e-12
GN_EPS_PER_CH = 1e-5
DECAY_LOG_SCALE = math.exp(-0.5)

V7X_LANES = 128
V7X_SUBLANES = 8
RWKV_HEAD_DIM = 64
GMLP_BLOCK = 128
STREAM_CHUNK = 64
SCAN_CHUNK = 64
SCAN_HEADS = 4
VMEM_LIMIT = 60 * 1024 * 1024


def _cparams(*sem):
    return pltpu.CompilerParams(dimension_semantics=sem, vmem_limit_bytes=VMEM_LIMIT)


def _nt(a, b):
    return lax.dot_general(a, b, (((1,), (1,)), ((), ())), preferred_element_type=F32)


def _mm(a, b, precision=None):
    return jnp.dot(a, b, precision=precision, preferred_element_type=F32)


def _mmb(a, b):
    return _mm(a.astype(BF16), b.astype(BF16))


def _rms_kernel(x_ref, g_ref, o_ref):
    x = x_ref[...]
    ms = jnp.mean(x * x, axis=-1, keepdims=True)
    o_ref[...] = (x * lax.rsqrt(ms + RMS_EPS) * g_ref[...]).astype(o_ref.dtype)


def rms_norm_bf16(x, g, tm):
    n, d = x.shape
    return pl.pallas_call(
        _rms_kernel,
        grid=(n // tm,),
        in_specs=[pl.BlockSpec((tm, d), lambda i: (i, 0)), pl.BlockSpec((1, d), lambda i: (0, 0))],
        out_specs=pl.BlockSpec((tm, d), lambda i: (i, 0)),
        out_shape=jax.ShapeDtypeStruct((n, d), BF16),
        compiler_params=_cparams("parallel"),
        name="rms_norm",
    )(x, g.reshape(1, d))


def _mm_kernel(a_ref, w_ref, o_ref):
    o_ref[...] = _mm(a_ref[...], w_ref[...].astype(a_ref.dtype)).astype(o_ref.dtype)


def matmul(a, w, tm, tn, name):
    m, k = a.shape
    n = w.shape[1]
    return pl.pallas_call(
        _mm_kernel,
        grid=(m // tm, pl.cdiv(n, tn)),
        in_specs=[pl.BlockSpec((tm, k), lambda i, j: (i, 0)), pl.BlockSpec((k, tn), lambda i, j: (0, j))],
        out_specs=pl.BlockSpec((tm, tn), lambda i, j: (i, j)),
        out_shape=jax.ShapeDtypeStruct((m, n), F32),
        compiler_params=_cparams("parallel", "arbitrary"),
        name=name,
    )(a, w)


def _mm2_kernel(a1_ref, a2_ref, w1_ref, w2_ref, o_ref):
    o_ref[...] = _mm(a1_ref[...], w1_ref[...]) + _mm(a2_ref[...], w2_ref[...])


def matmul_concat2(a1, a2, w, tm, tn, name):
    m, k1 = a1.shape
    k2 = a2.shape[1]
    n = w.shape[1]
    assert k1 == k2
    return pl.pallas_call(
        _mm2_kernel,
        grid=(m // tm, n // tn),
        in_specs=[pl.BlockSpec((tm, k1), lambda i, j: (i, 0)),
                  pl.BlockSpec((tm, k2), lambda i, j: (i, 0)),
                  pl.BlockSpec((k1, tn), lambda i, j: (0, j)),
                  pl.BlockSpec((k2, tn), lambda i, j: (1, j))],
        out_specs=pl.BlockSpec((tm, tn), lambda i, j: (i, j)),
        out_shape=jax.ShapeDtypeStruct((m, n), F32),
        compiler_params=_cparams("parallel", "arbitrary"),
        name=name,
    )(a1, a2, w, w)


def _gmlp_kernel(u_ref, v_ref, lng_ref, lnb_ref, ws_ref, bs_ref, o_ref, *, heads, nblk):
    zv = jax.nn.gelu(v_ref[...])
    mean = jnp.mean(zv, axis=-1, keepdims=True)
    xc = zv - mean
    var = jnp.mean(xc * xc, axis=-1, keepdims=True)
    vn = (xc * lax.rsqrt(var + LN_EPS) * lng_ref[...] + lnb_ref[...]).astype(BF16)
    row = lax.broadcasted_iota(jnp.int32, (GMLP_BLOCK, GMLP_BLOCK), 0) // STREAM_CHUNK
    col = lax.broadcasted_iota(jnp.int32, (GMLP_BLOCK, GMLP_BLOCK), 1) // STREAM_CHUNK
    causal = col <= row
    for h in range(heads):
        wm = jnp.where(causal, ws_ref[h], 0.0).astype(BF16)
        bias = bs_ref[h]
        cs = slice(h * V7X_LANES, (h + 1) * V7X_LANES)
        for n in range(nblk):
            rs = slice(n * GMLP_BLOCK, (n + 1) * GMLP_BLOCK)
            mixed = _mm(wm, vn[rs, cs]) + bias
            o_ref[rs, cs] = (jax.nn.gelu(u_ref[rs, cs]) * mixed).astype(o_ref.dtype)


def gmlp_gating(p_main, ln_g, ln_b, ws, bs, rows):
    n = p_main.shape[0]
    heads = ws.shape[0]
    gw = heads * V7X_LANES
    assert ws.shape[1:] == (GMLP_BLOCK, GMLP_BLOCK) and rows % GMLP_BLOCK == 0
    kern = functools.partial(_gmlp_kernel, heads=heads, nblk=rows // GMLP_BLOCK)
    return pl.pallas_call(
        kern,
        grid=(n // rows,),
        in_specs=[pl.BlockSpec((rows, gw), lambda i: (i, 0)),
                  pl.BlockSpec((rows, gw), lambda i: (i, 1)),
                  pl.BlockSpec((1, gw), lambda i: (0, 0)),
                  pl.BlockSpec((1, gw), lambda i: (0, 0)),
                  pl.BlockSpec((heads, GMLP_BLOCK, GMLP_BLOCK), lambda i: (0, 0, 0)),
                  pl.BlockSpec((heads, GMLP_BLOCK, 1), lambda i: (0, 0, 0))],
        out_specs=pl.BlockSpec((rows, gw), lambda i: (i, 0)),
        out_shape=jax.ShapeDtypeStruct((n, gw), BF16),
        compiler_params=_cparams("parallel"),
        name="gmlp_gating",
    )(p_main, p_main, ln_g.reshape(1, gw), ln_b.reshape(1, gw), ws, bs[:, :, None])


def _rwkv_kernel(r_ref, k_ref, v_ref, t0_ref, t1_ref, t2_ref, rh_ref, kh_ref, vh_ref, th0_ref, th1_ref, th2_ref,
                 mur_ref, muk_ref, muv_ref, mut_ref, w0_ref, wup_ref, a0_ref, aup_ref, gup_ref,
                 kk_ref, ka_ref, rk_ref, lng_ref, lnb_ref, o_ref, state_ref, *, heads, nchunks):
    C = SCAN_CHUNK
    N = RWKV_HEAD_DIM
    L = heads * N
    first = pl.program_id(2) == 0

    @pl.when(first)
    def _():
        state_ref[...] = jnp.zeros_like(state_ref)

    def shifted(ref, halo_ref, mu):
        p = ref[...]
        halo = jnp.where(first, 0.0, halo_ref[V7X_SUBLANES - 1:V7X_SUBLANES, :])
        rows = lax.broadcasted_iota(jnp.int32, p.shape, 0)
        prev = jnp.where(rows == 0, halo, pltpu.roll(p, 1, 0))
        return p + (prev - p) * mu

    r_all = shifted(r_ref, rh_ref, mur_ref[...])
    k_all = shifted(k_ref, kh_ref, muk_ref[...])
    v_all = shifted(v_ref, vh_ref, muv_ref[...])
    mut = mut_ref[...]
    W = V7X_LANES
    lora_in = shifted(t0_ref, th0_ref, mut[:, :W])
    xg = jnp.concatenate([shifted(t1_ref, th1_ref, mut[:, W:2 * W]),
                          shifted(t2_ref, th2_ref, mut[:, 2 * W:])], axis=1)

    zw = w0_ref[...] + _mm(jnp.tanh(lora_in).astype(BF16), wup_ref[...])
    lw_all = jax.nn.sigmoid(zw) * (-DECAY_LOG_SCALE)
    a_all = jax.nn.sigmoid(a0_ref[...] + _mm(lora_in.astype(BF16), aup_ref[...]))
    g_all = _mm(jax.nn.sigmoid(xg).astype(BF16), gup_ref[...])

    r2 = lax.broadcasted_iota(jnp.int32, (L, L), 0)
    c2 = lax.broadcasted_iota(jnp.int32, (L, L), 1)
    bd_mask = (r2 // N) == (c2 // N)
    eye2 = r2 == c2
    ones_bd = bd_mask.astype(BF16)

    def head_sum(x):
        return _mm(x.astype(BF16), ones_bd)

    kk_all = k_all * kk_ref[...]
    kk_all = kk_all / jnp.maximum(jnp.sqrt(head_sum(kk_all * kk_all)), L2_EPS)
    k2_all = k_all * (1.0 + (a_all - 1.0) * ka_ref[...])
    bv_all = head_sum(r_all * k2_all * rk_ref[...]) * v_all
    beta_all = kk_all * a_all

    ti = lax.broadcasted_iota(jnp.int32, (C, C), 0)
    si = lax.broadcasted_iota(jnp.int32, (C, C), 1)
    tri = (ti >= si).astype(F32)
    lane = lax.broadcasted_iota(jnp.int32, (C, L), 1)
    rowi = lax.broadcasted_iota(jnp.int32, (C, L), 0)
    scol = lane % N
    strict = rowi > scol
    incl = rowi >= scol
    eye_p = (rowi == scol).astype(F32)
    head_sel = [lane // N == h for h in range(heads)]

    def sm(x):
        xb = x.astype(BF16)
        zero = jnp.zeros_like(xb)
        return jnp.concatenate([jnp.where(sel, xb, zero) for sel in head_sel], axis=0)

    chunks = range(nchunks)

    def each(fn, *lists):
        return [fn(*xs) for xs in zip(*lists)]

    rows = [slice(c * C, (c + 1) * C) for c in chunks]
    lw = [lw_all[rs] for rs in rows]
    cm = [_mm(tri, x, HIGHEST) for x in lw]
    r = [r_all[rs] for rs in rows]
    k = [k2_all[rs] for rs in rows]
    v = [v_all[rs] for rs in rows]
    kk = [kk_all[rs] for rs in rows]
    beta = [beta_all[rs] for rs in rows]
    ginv = each(lambda m: jnp.exp(-m), cm)
    a_t = each(lambda kk_, m, w: -kk_ * jnp.exp(m - w), kk, cm, lw)
    b_t = each(jnp.multiply, beta, ginv)
    k_t = each(jnp.multiply, k, ginv)
    r_t = each(lambda r_, m: r_ * jnp.exp(m), r, cm)
    dec_end = each(lambda m: jnp.exp(jnp.broadcast_to(m[C - 1:C, :], (C, L)) - m), cm)
    b_h = each(jnp.multiply, beta, dec_end)
    k_h = each(jnp.multiply, k, dec_end)

    sc = each(lambda a_, r_, b_, k_: _nt(jnp.concatenate([a_, r_], axis=0).astype(BF16),
                                         jnp.concatenate([sm(b_), sm(k_)], axis=0)),
              a_t, r_t, b_t, k_t)
    ab = [x[:C, :L] for x in sc]
    l_ak = [jnp.where(strict, x[:C, L:], 0.0) for x in sc]
    p_rb = [jnp.where(incl, x[C:, :L], 0.0) for x in sc]
    p_rk = [jnp.where(incl, x[C:, L:], 0.0) for x in sc]

    base = 8
    d = [jnp.where(strict & (rowi // base == scol // base), x, 0.0) for x in ab]
    t_inv = [eye_p + x for x in d]
    p2 = each(lambda x: _mmb(x, sm(x)), d)
    both = each(lambda p, t: _mmb(p, jnp.concatenate([sm(p), sm(t)], axis=1)), p2, t_inv)
    t_inv = each(lambda t, bo: t + bo[:, L:], t_inv, both)
    t_inv = each(lambda t, bo: t + _mmb(bo[:, :L], sm(t)), t_inv, both)
    b = base
    while b < C:
        blk_off = (rowi // (2 * b) == scol // (2 * b)) & (rowi // b > scol // b)
        ot = each(lambda x, t: _mmb(jnp.where(blk_off, x, 0.0), sm(t)), ab, t_inv)
        t_inv = each(lambda t, o: t + _mmb(t, sm(o)), t_inv, ot)
        b *= 2

    sm_v = each(sm, v)
    x1 = each(_mmb, l_ak, sm_v)
    wv = each(lambda t, a_, x_: _mmb(t, jnp.concatenate([sm(a_), sm(x_)], axis=1)), t_inv, a_t, x1)
    w_t = [x[:, :L] for x in wv]
    v_t = [x[:, L:] for x in wv]
    q = each(lambda r_, p, w: (r_ + _mmb(p, sm(w))).astype(BF16), r_t, p_rb, w_t)
    y0 = each(lambda pb, pk, vt, sv: _mmb(jnp.concatenate([pb, pk], axis=1), jnp.concatenate([sm(vt), sv], axis=0)),
              p_rb, p_rk, v_t, sm_v)
    b_ht = [x.T for x in b_h]
    k_ht = [x.T for x in k_h]
    m_bd = each(lambda bt, w: jnp.where(bd_mask, _mmb(bt, w), 0.0).astype(BF16), b_ht, w_t)
    n_bd = each(lambda bt, kt, vt, v_: jnp.where(bd_mask, _mmb(jnp.concatenate([bt, kt], axis=1),
                                                                jnp.concatenate([vt, v_], axis=0)), 0.0),
                b_ht, k_ht, v_t, v)
    gcol = each(lambda m: jnp.sum(jnp.where(eye2, jnp.exp(jnp.broadcast_to(m[C - 1:C, :], (L, L))), 0.0),
                                  axis=1, keepdims=True), cm)

    S = state_ref[...]
    ys = []
    for c in chunks:
        sb = S.astype(BF16)
        ys.append(_mm(q[c], sb) + y0[c])
        S = gcol[c] * S + _mm(m_bd[c], sb) + n_bd[c]
    state_ref[...] = S

    y = jnp.concatenate(ys, axis=0)
    inv_n = 1.0 / N
    yc = y - head_sum(y) * inv_n
    var = head_sum(yc * yc) * inv_n
    yn = yc * lax.rsqrt(var + N * GN_EPS_PER_CH) * lng_ref[...] + lnb_ref[...]
    o_ref[...] = ((yn + bv_all) * g_all).astype(o_ref.dtype)


def rwkv_time_mix(p, mu, w0, w_up, a0, a_up, g_up, k_k, k_a, r_k, lnx_g, lnx_b,
                  batch, seq, tc, first_col):
    n = p.shape[0]
    rw = w0.shape[0]
    L = SCAN_HEADS * RWKV_HEAD_DIM
    dl, il = w_up.shape[0], a_up.shape[0]
    W = V7X_LANES
    tw = p.shape[1] - first_col - 3 * rw
    tb = (first_col + 3 * rw) // W
    assert tw == 3 * W and g_up.shape[0] == 2 * W and (first_col + 3 * rw) % W == 0
    assert dl + il == V7X_LANES and seq % tc == 0 and tc % SCAN_CHUNK == 0 and rw % L == 0 and first_col % L == 0
    nt = seq // tc
    ng = rw // L
    cb = first_col // L
    sub = V7X_SUBLANES
    row = lambda a: a.reshape(1, -1)
    mur, muk, muv, mut = mu[:rw], mu[rw:2 * rw], mu[2 * rw:3 * rw], mu[3 * rw:]
    wup_pad = jnp.concatenate([w_up, jnp.zeros((il, rw), w_up.dtype)], axis=0).astype(BF16)
    aup_pad = jnp.concatenate([jnp.zeros((dl, rw), a_up.dtype), a_up], axis=0).astype(BF16)
    gup = g_up.astype(BF16)

    def halo_row(b, c):
        return jnp.maximum((b * nt + c) * (tc // sub) - 1, 0)

    blk = lambda i: pl.BlockSpec((tc, L), lambda b, g, c: (b * nt + c, cb + i * ng + g))
    halo = lambda i: pl.BlockSpec((sub, L), lambda b, g, c: (halo_row(b, c), cb + i * ng + g))
    tail = lambda j: pl.BlockSpec((tc, W), lambda b, g, c: (b * nt + c, tb + j))
    tail_halo = lambda j: pl.BlockSpec((sub, W), lambda b, g, c: (halo_row(b, c), tb + j))
    vec = pl.BlockSpec((1, L), lambda b, g, c: (0, g))
    lora = lambda a: pl.BlockSpec((a.shape[0], L), lambda b, g, c: (0, g))
    return pl.pallas_call(
        functools.partial(_rwkv_kernel, heads=SCAN_HEADS, nchunks=tc // SCAN_CHUNK),
        grid=(batch, ng, nt),
        in_specs=[blk(0), blk(1), blk(2), tail(0), tail(1), tail(2),
                  halo(0), halo(1), halo(2), tail_halo(0), tail_halo(1), tail_halo(2),
                  vec, vec, vec, pl.BlockSpec((1, tw), lambda b, g, c: (0, 0)),
                  vec, lora(wup_pad), vec, lora(aup_pad), lora(gup), vec, vec, vec, vec, vec],
        out_specs=pl.BlockSpec((tc, L), lambda b, g, c: (b * nt + c, g)),
        out_shape=jax.ShapeDtypeStruct((n, rw), BF16),
        scratch_shapes=[pltpu.VMEM((L, L), F32)],
        compiler_params=_cparams("parallel", "parallel", "arbitrary"),
        name="rwkv_time_mix",
    )(*([p] * 12),
      row(mur), row(muk), row(muv), row(mut), row(w0), wup_pad, row(a0), aup_pad, gup,
      row(k_k), row(k_a), row(r_k), row(lnx_g), row(lnx_b))


def _post_mix_kernel(mix_ref, x_ref, gpost_ref, gpre_ref, x1_ref, xn_ref):
    m = mix_ref[...]
    x1 = x_ref[...] + m * lax.rsqrt(jnp.mean(m * m, axis=-1, keepdims=True) + RMS_EPS) * gpost_ref[...]
    x1_ref[...] = x1
    xn_ref[...] = (x1 * lax.rsqrt(jnp.mean(x1 * x1, axis=-1, keepdims=True) + RMS_EPS)
                   * gpre_ref[...]).astype(xn_ref.dtype)


def post_mix(mix, x, g_post, g_pre, tm):
    n, d = x.shape
    blk = pl.BlockSpec((tm, d), lambda i: (i, 0))
    vec = pl.BlockSpec((1, d), lambda i: (0, 0))
    return pl.pallas_call(
        _post_mix_kernel,
        grid=(n // tm,),
        in_specs=[blk, blk, vec, vec],
        out_specs=[blk, blk],
        out_shape=[jax.ShapeDtypeStruct((n, d), F32), jax.ShapeDtypeStruct((n, d), BF16)],
        compiler_params=_cparams("parallel"),
        name="post_mix",
    )(mix, x, g_post.reshape(1, d), g_pre.reshape(1, d))


def _ffn_kernel(xn_ref, w1_ref, w2_ref, o_ref):
    @pl.when(pl.program_id(1) == 0)
    def _():
        o_ref[...] = jnp.zeros_like(o_ref)

    h = jnp.maximum(_mm(xn_ref[...], w1_ref[...]), 0.0)
    o_ref[...] += _mm((h * h).astype(BF16), w2_ref[...])


def ffn(xn, w1, w2, tm, tf):
    n, d = xn.shape
    f = w1.shape[1]
    return pl.pallas_call(
        _ffn_kernel,
        grid=(n // tm, f // tf),
        in_specs=[pl.BlockSpec((tm, d), lambda i, j: (i, 0), pipeline_mode=pl.Buffered(1)),
                  pl.BlockSpec((d, tf), lambda i, j: (0, j)),
                  pl.BlockSpec((tf, d), lambda i, j: (j, 0))],
        out_specs=pl.BlockSpec((tm, d), lambda i, j: (i, 0)),
        out_shape=jax.ShapeDtypeStruct((n, d), F32),
        compiler_params=_cparams("parallel", "arbitrary"),
        name="ffn",
    )(xn, w1, w2)


def _post_ffn_kernel(f_ref, x_ref, g_ref, o_ref):
    f = f_ref[...]
    o_ref[...] = x_ref[...] + f * lax.rsqrt(jnp.mean(f * f, axis=-1, keepdims=True) + RMS_EPS) * g_ref[...]


def post_ffn(f, x1, g, tm):
    n, d = x1.shape
    blk = pl.BlockSpec((tm, d), lambda i: (i, 0))
    return pl.pallas_call(
        _post_ffn_kernel,
        grid=(n // tm,),
        in_specs=[blk, blk, pl.BlockSpec((1, d), lambda i: (0, 0))],
        out_specs=blk,
        out_shape=jax.ShapeDtypeStruct((n, d), F32),
        compiler_params=_cparams("parallel"),
        name="post_ffn",
    )(f, x1, g.reshape(1, d))


def _tile(n, want):
    if n <= want:
        return n
    t = want - want % V7X_LANES
    while n % t:
        t -= V7X_LANES
    assert t > 0, (n, want)
    return t


def kernel(x, pre_mix_g, w_in, tshift_mu, gmlp_ln_g, gmlp_ln_b, gmlp_ws, gmlp_bs, decay_w0, decay_up, iclr_a0, iclr_up, gate_up, k_k, k_a, r_k, lnx_g, lnx_b, w_out, post_mix_g, pre_ffn_g, w_ff1, w_ff2, post_ffn_g):
    batch, seq, d = x.shape
    depth = w_in.shape[0]
    n = batch * seq
    gw = gmlp_ws.shape[1] * V7X_LANES
    rw = decay_w0.shape[1]
    xf = x.reshape(n, d)
    for l in range(depth):
        h = rms_norm_bf16(xf, pre_mix_g[l], _tile(n, 512))
        p = matmul(h, w_in[l], _tile(n, 1024), 768, "w_in")

        y_a = gmlp_gating(p, gmlp_ln_g[l], gmlp_ln_b[l], gmlp_ws[l], gmlp_bs[l], _tile(seq, 512))
        y_b = rwkv_time_mix(p, tshift_mu[l], decay_w0[l], decay_up[l], iclr_a0[l], iclr_up[l],
                            gate_up[l], k_k[l], k_a[l], r_k[l].reshape(-1), lnx_g[l], lnx_b[l],
                            batch, seq, _tile(seq, 512), first_col=2 * gw)

        mix = matmul_concat2(y_a, y_b, w_out[l].astype(BF16), _tile(n, 1024), _tile(d, 1024), "w_out")
        x1, xn = post_mix(mix, xf, post_mix_g[l], pre_ffn_g[l], _tile(n, 256))
        f = ffn(xn, w_ff1[l].astype(BF16), w_ff2[l].astype(BF16), _tile(n, 1024), _tile(w_ff1.shape[2], 512))
        xf = post_ffn(f, x1, post_ffn_g[l], _tile(n, 256))
    return xf.reshape(batch, seq, d)
```

```python
import functools
import math

import jax
import jax.numpy as jnp
from jax import lax
from jax.experimental import pallas as pl
from jax.experimental.pallas import tpu as pltpu

F32 = jnp.float32
BF16 = jnp.bfloat16

RMS_EPS = 1e-6
LN_EPS = 1e-5
L2_EPS = 1e-12
GN_EPS_PER_CH = 1e-5
DECAY_LOG_SCALE = math.exp(-0.5)

V7X_LANES = 128
V7X_SUBLANES = 8
RWKV_HEAD_DIM = 64
GMLP_BLOCK = 128
STREAM_CHUNK = 64
SCAN_CHUNK = 64
SCAN_HEADS = 4
VMEM_LIMIT = 60 * 1024 * 1024


def _cparams(*sem):
    return pltpu.CompilerParams(dimension_semantics=sem, vmem_limit_bytes=VMEM_LIMIT)


def _nt(a, b):
    return lax.dot_general(a, b, (((1,), (1,)), ((), ())), preferred_element_type=F32)


def _mm(a, b, precision=None):
    return jnp.dot(a, b, precision=precision, preferred_element_type=F32)


def _mmb(a, b):
    return _mm(a.astype(BF16), b.astype(BF16))


def _rms_kernel(x_ref, g_ref, o_ref):
    x = x_ref[...]
    ms = jnp.mean(x * x, axis=-1, keepdims=True)
    o_ref[...] = (x * lax.rsqrt(ms + RMS_EPS) * g_ref[...]).astype(o_ref.dtype)


def rms_norm_bf16(x, g, tm):
    n, d = x.shape
    return pl.pallas_call(
        _rms_kernel,
        grid=(n // tm,),
        in_specs=[pl.BlockSpec((tm, d), lambda i: (i, 0)), pl.BlockSpec((1, d), lambda i: (0, 0))],
        out_specs=pl.BlockSpec((tm, d), lambda i: (i, 0)),
        out_shape=jax.ShapeDtypeStruct((n, d), BF16),
        compiler_params=_cparams("parallel"),
        name="rms_norm",
    )(x, g.reshape(1, d))


def _mm_kernel(a_ref, w_ref, o_ref):
    o_ref[...] = _mm(a_ref[...], w_ref[...].astype(a_ref.dtype)).astype(o_ref.dtype)


def matmul(a, w, tm, tn, name):
    m, k = a.shape
    n = w.shape[1]
    return pl.pallas_call(
        _mm_kernel,
        grid=(m // tm, pl.cdiv(n, tn)),
        in_specs=[pl.BlockSpec((tm, k), lambda i, j: (i, 0)), pl.BlockSpec((k, tn), lambda i, j: (0, j))],
        out_specs=pl.BlockSpec((tm, tn), lambda i, j: (i, j)),
        out_shape=jax.ShapeDtypeStruct((m, n), F32),
        compiler_params=_cparams("parallel", "arbitrary"),
        name=name,
    )(a, w)


def _mm2_kernel(a1_ref, a2_ref, w1_ref, w2_ref, o_ref):
    o_ref[...] = _mm(a1_ref[...], w1_ref[...]) + _mm(a2_ref[...], w2_ref[...])


def matmul_concat2(a1, a2, w, tm, tn, name):
    m, k1 = a1.shape
    k2 = a2.shape[1]
    n = w.shape[1]
    assert k1 == k2
    return pl.pallas_call(
        _mm2_kernel,
        grid=(m // tm, n // tn),
        in_specs=[pl.BlockSpec((tm, k1), lambda i, j: (i, 0)),
                  pl.BlockSpec((tm, k2), lambda i, j: (i, 0)),
                  pl.BlockSpec((k1, tn), lambda i, j: (0, j)),
                  pl.BlockSpec((k2, tn), lambda i, j: (1, j))],
        out_specs=pl.BlockSpec((tm, tn), lambda i, j: (i, j)),
        out_shape=jax.ShapeDtypeStruct((m, n), F32),
        compiler_params=_cparams("parallel", "arbitrary"),
        name=name,
    )(a1, a2, w, w)


def _gmlp_kernel(u_ref, v_ref, lng_ref, lnb_ref, ws_ref, bs_ref, o_ref, *, heads, nblk):
    zv = jax.nn.gelu(v_ref[...])
    mean = jnp.mean(zv, axis=-1, keepdims=True)
    xc = zv - mean
    var = jnp.mean(xc * xc, axis=-1, keepdims=True)
    vn = (xc * lax.rsqrt(var + LN_EPS) * lng_ref[...] + lnb_ref[...]).astype(BF16)
    row = lax.broadcasted_iota(jnp.int32, (GMLP_BLOCK, GMLP_BLOCK), 0) // STREAM_CHUNK
    col = lax.broadcasted_iota(jnp.int32, (GMLP_BLOCK, GMLP_BLOCK), 1) // STREAM_CHUNK
    causal = col <= row
    for h in range(heads):
        wm = jnp.where(causal, ws_ref[h], 0.0).astype(BF16)
        bias = bs_ref[h]
        cs = slice(h * V7X_LANES, (h + 1) * V7X_LANES)
        for n in range(nblk):
            rs = slice(n * GMLP_BLOCK, (n + 1) * GMLP_BLOCK)
            mixed = _mm(wm, vn[rs, cs]) + bias
            o_ref[rs, cs] = (jax.nn.gelu(u_ref[rs, cs]) * mixed).astype(o_ref.dtype)


def gmlp_gating(p_main, ln_g, ln_b, ws, bs, rows):
    n = p_main.shape[0]
    heads = ws.shape[0]
    gw = heads * V7X_LANES
    assert ws.shape[1:] == (GMLP_BLOCK, GMLP_BLOCK) and rows % GMLP_BLOCK == 0
    kern = functools.partial(_gmlp_kernel, heads=heads, nblk=rows // GMLP_BLOCK)
    return pl.pallas_call(
        kern,
        grid=(n // rows,),
        in_specs=[pl.BlockSpec((rows, gw), lambda i: (i, 0)),
                  pl.BlockSpec((rows, gw), lambda i: (i, 1)),
                  pl.BlockSpec((1, gw), lambda i: (0, 0)),
                  pl.BlockSpec((1, gw), lambda i: (0, 0)),
                  pl.BlockSpec((heads, GMLP_BLOCK, GMLP_BLOCK), lambda i: (0, 0, 0)),
                  pl.BlockSpec((heads, GMLP_BLOCK, 1), lambda i: (0, 0, 0))],
        out_specs=pl.BlockSpec((rows, gw), lambda i: (i, 0)),
        out_shape=jax.ShapeDtypeStruct((n, gw), BF16),
        compiler_params=_cparams("parallel"),
        name="gmlp_gating",
    )(p_main, p_main, ln_g.reshape(1, gw), ln_b.reshape(1, gw), ws, bs[:, :, None])


def _rwkv_kernel(r_ref, k_ref, v_ref, t0_ref, t1_ref, t2_ref, rh_ref, kh_ref, vh_ref, th0_ref, th1_ref, th2_ref,
                 mur_ref, muk_ref, muv_ref, mut_ref, w0_ref, wup_ref, a0_ref, aup_ref, gup_ref,
                 kk_ref, ka_ref, rk_ref, lng_ref, lnb_ref, o_ref, state_ref, *, heads, nchunks):
    C = SCAN_CHUNK
    N = RWKV_HEAD_DIM
    L = heads * N
    first = pl.program_id(2) == 0

    @pl.when(first)
    def _():
        state_ref[...] = jnp.zeros_like(state_ref)

    def shifted(ref, halo_ref, mu):
        p = ref[...]
        halo = jnp.where(first, 0.0, halo_ref[V7X_SUBLANES - 1:V7X_SUBLANES, :])
        rows = lax.broadcasted_iota(jnp.int32, p.shape, 0)
        prev = jnp.where(rows == 0, halo, pltpu.roll(p, 1, 0))
        return p + (prev - p) * mu

    r_all = shifted(r_ref, rh_ref, mur_ref[...])
    k_all = shifted(k_ref, kh_ref, muk_ref[...])
    v_all = shifted(v_ref, vh_ref, muv_ref[...])
    mut = mut_ref[...]
    W = V7X_LANES
    lora_in = shifted(t0_ref, th0_ref, mut[:, :W])
    xg = jnp.concatenate([shifted(t1_ref, th1_ref, mut[:, W:2 * W]),
                          shifted(t2_ref, th2_ref, mut[:, 2 * W:])], axis=1)

    zw = w0_ref[...] + _mm(jnp.tanh(lora_in).astype(BF16), wup_ref[...])
    lw_all = jax.nn.sigmoid(zw) * (-DECAY_LOG_SCALE)
    a_all = jax.nn.sigmoid(a0_ref[...] + _mm(lora_in.astype(BF16), aup_ref[...]))
    g_all = _mm(jax.nn.sigmoid(xg).astype(BF16), gup_ref[...])

    r2 = lax.broadcasted_iota(jnp.int32, (L, L), 0)
    c2 = lax.broadcasted_iota(jnp.int32, (L, L), 1)
    bd_mask = (r2 // N) == (c2 // N)
    eye2 = r2 == c2
    ones_bd = bd_mask.astype(BF16)

    def head_sum(x):
        return _mm(x.astype(BF16), ones_bd)

    kk_all = k_all * kk_ref[...]
    kk_all = kk_all / jnp.maximum(jnp.sqrt(head_sum(kk_all * kk_all)), L2_EPS)
    k2_all = k_all * (1.0 + (a_all - 1.0) * ka_ref[...])
    bv_all = head_sum(r_all * k2_all * rk_ref[...]) * v_all
    beta_all = kk_all * a_all

    ti = lax.broadcasted_iota(jnp.int32, (C, C), 0)
    si = lax.broadcasted_iota(jnp.int32, (C, C), 1)
    tri = (ti >= si).astype(BF16)
    lane = lax.broadcasted_iota(jnp.int32, (C, L), 1)
    rowi = lax.broadcasted_iota(jnp.int32, (C, L), 0)
    scol = lane % N
    strict = rowi > scol
    incl = rowi >= scol
    eye_p = (rowi == scol).astype(F32)
    head_sel = [lane // N == h for h in range(heads)]

    def sm(x):
        xb = x.astype(BF16)
        zero = jnp.zeros_like(xb)
        return jnp.concatenate([jnp.where(sel, xb, zero) for sel in head_sel], axis=0)

    chunks = range(nchunks)

    def each(fn, *lists):
        return [fn(*xs) for xs in zip(*lists)]

    rows = [slice(c * C, (c + 1) * C) for c in chunks]
    lw = [lw_all[rs] for rs in rows]
    def cumsum_rows(x):
        h1 = x.astype(BF16)
        r1 = x - h1.astype(F32)
        h2 = r1.astype(BF16)
        h3 = (r1 - h2.astype(F32)).astype(BF16)
        return _mm(tri, h1) + _mm(tri, h2) + _mm(tri, h3)

    cm = [cumsum_rows(x) for x in lw]
    r = [r_all[rs] for rs in rows]
    k = [k2_all[rs] for rs in rows]
    v = [v_all[rs] for rs in rows]
    kk = [kk_all[rs] for rs in rows]
    beta = [beta_all[rs] for rs in rows]
    ginv = each(lambda m: jnp.exp(-m), cm)
    a_t = each(lambda kk_, m, w: -kk_ * jnp.exp(m - w), kk, cm, lw)
    b_t = each(jnp.multiply, beta, ginv)
    k_t = each(jnp.multiply, k, ginv)
    r_t = each(lambda r_, m: r_ * jnp.exp(m), r, cm)
    dec_end = each(lambda m: jnp.exp(jnp.broadcast_to(m[C - 1:C, :], (C, L)) - m), cm)
    b_h = each(jnp.multiply, beta, dec_end)
    k_h = each(jnp.multiply, k, dec_end)

    sc = each(lambda a_, r_, b_, k_: _nt(jnp.concatenate([a_, r_], axis=0).astype(BF16),
                                         jnp.concatenate([sm(b_), sm(k_)], axis=0)),
              a_t, r_t, b_t, k_t)
    ab = [x[:C, :L] for x in sc]
    l_ak = [jnp.where(strict, x[:C, L:], 0.0) for x in sc]
    p_rb = [jnp.where(incl, x[C:, :L], 0.0) for x in sc]
    p_rk = [jnp.where(incl, x[C:, L:], 0.0) for x in sc]

    base = 8
    d = [jnp.where(strict & (rowi // base == scol // base), x, 0.0) for x in ab]
    t_inv = [eye_p + x for x in d]
    p2 = each(lambda x: _mmb(x, sm(x)), d)
    both = each(lambda p, t: _mmb(p, jnp.concatenate([sm(p), sm(t)], axis=1)), p2, t_inv)
    t_inv = each(lambda t, bo: t + bo[:, L:], t_inv, both)
    t_inv = each(lambda t, bo: t + _mmb(bo[:, :L], sm(t)), t_inv, both)
    b = base
    while b < C:
        blk_off = (rowi // (2 * b) == scol // (2 * b)) & (rowi // b > scol // b)
        ot = each(lambda x, t: _mmb(jnp.where(blk_off, x, 0.0), sm(t)), ab, t_inv)
        t_inv = each(lambda t, o: t + _mmb(t, sm(o)), t_inv, ot)
        b *= 2

    sm_v = each(sm, v)
    x1 = each(_mmb, l_ak, sm_v)
    wv = each(lambda t, a_, x_: _mmb(t, jnp.concatenate([sm(a_), sm(x_)], axis=1)), t_inv, a_t, x1)
    w_t = [x[:, :L] for x in wv]
    v_t = [x[:, L:] for x in wv]
    q = each(lambda r_, p, w: (r_ + _mmb(p, sm(w))).astype(BF16), r_t, p_rb, w_t)
    y0 = each(lambda pb, pk, vt, sv: _mmb(jnp.concatenate([pb, pk], axis=1), jnp.concatenate([sm(vt), sv], axis=0)),
              p_rb, p_rk, v_t, sm_v)
    b_ht = [x.T for x in b_h]
    k_ht = [x.T for x in k_h]
    m_bd = each(lambda bt, w: jnp.where(bd_mask, _mmb(bt, w), 0.0).astype(BF16), b_ht, w_t)
    n_bd = each(lambda bt, kt, vt, v_: jnp.where(bd_mask, _mmb(jnp.concatenate([bt, kt], axis=1),
                                                                jnp.concatenate([vt, v_], axis=0)), 0.0),
                b_ht, k_ht, v_t, v)
    gcol = each(lambda m: jnp.sum(jnp.where(eye2, jnp.exp(jnp.broadcast_to(m[C - 1:C, :], (L, L))), 0.0),
                                  axis=1, keepdims=True), cm)

    S = state_ref[...]
    ys = []
    for c in chunks:
        sb = S.astype(BF16)
        ys.append(_mm(q[c], sb) + y0[c])
        S = gcol[c] * S + _mm(m_bd[c], sb) + n_bd[c]
    state_ref[...] = S

    y = jnp.concatenate(ys, axis=0)
    inv_n = 1.0 / N
    yc = y - head_sum(y) * inv_n
    var = head_sum(yc * yc) * inv_n
    yn = yc * lax.rsqrt(var + N * GN_EPS_PER_CH) * lng_ref[...] + lnb_ref[...]
    o_ref[...] = ((yn + bv_all) * g_all).astype(o_ref.dtype)


def rwkv_time_mix(p, mu, w0, w_up, a0, a_up, g_up, k_k, k_a, r_k, lnx_g, lnx_b,
                  batch, seq, tc, first_col):
    n = p.shape[0]
    rw = w0.shape[0]
    L = SCAN_HEADS * RWKV_HEAD_DIM
    dl, il = w_up.shape[0], a_up.shape[0]
    W = V7X_LANES
    tw = p.shape[1] - first_col - 3 * rw
    tb = (first_col + 3 * rw) // W
    assert tw == 3 * W and g_up.shape[0] == 2 * W and (first_col + 3 * rw) % W == 0
    assert dl + il == V7X_LANES and seq % tc == 0 and tc % SCAN_CHUNK == 0 and rw % L == 0 and first_col % L == 0
    nt = seq // tc
    ng = rw // L
    cb = first_col // L
    sub = V7X_SUBLANES
    row = lambda a: a.reshape(1, -1)
    mur, muk, muv, mut = mu[:rw], mu[rw:2 * rw], mu[2 * rw:3 * rw], mu[3 * rw:]
    wup_pad = jnp.concatenate([w_up, jnp.zeros((il, rw), w_up.dtype)], axis=0).astype(BF16)
    aup_pad = jnp.concatenate([jnp.zeros((dl, rw), a_up.dtype), a_up], axis=0).astype(BF16)
    gup = g_up.astype(BF16)

    def halo_row(b, c):
        return jnp.maximum((b * nt + c) * (tc // sub) - 1, 0)

    blk = lambda i: pl.BlockSpec((tc, L), lambda b, g, c: (b * nt + c, cb + i * ng + g))
    halo = lambda i: pl.BlockSpec((sub, L), lambda b, g, c: (halo_row(b, c), cb + i * ng + g))
    tail = lambda j: pl.BlockSpec((tc, W), lambda b, g, c: (b * nt + c, tb + j))
    tail_halo = lambda j: pl.BlockSpec((sub, W), lambda b, g, c: (halo_row(b, c), tb + j))
    vec = pl.BlockSpec((1, L), lambda b, g, c: (0, g))
    lora = lambda a: pl.BlockSpec((a.shape[0], L), lambda b, g, c: (0, g))
    return pl.pallas_call(
        functools.partial(_rwkv_kernel, heads=SCAN_HEADS, nchunks=tc // SCAN_CHUNK),
        grid=(batch, ng, nt),
        in_specs=[blk(0), blk(1), blk(2), tail(0), tail(1), tail(2),
                  halo(0), halo(1), halo(2), tail_halo(0), tail_halo(1), tail_halo(2),
                  vec, vec, vec, pl.BlockSpec((1, tw), lambda b, g, c: (0, 0)),
                  vec, lora(wup_pad), vec, lora(aup_pad), lora(gup), vec, vec, vec, vec, vec],
        out_specs=pl.BlockSpec((tc, L), lambda b, g, c: (b * nt + c, g)),
        out_shape=jax.ShapeDtypeStruct((n, rw), BF16),
        scratch_shapes=[pltpu.VMEM((L, L), F32)],
        compiler_params=_cparams("parallel", "parallel", "arbitrary"),
        name="rwkv_time_mix",
    )(*([p] * 12),
      row(mur), row(muk), row(muv), row(mut), row(w0), wup_pad, row(a0), aup_pad, gup,
      row(k_k), row(k_a), row(r_k), row(lnx_g), row(lnx_b))


def _post_mix_kernel(mix_ref, x_ref, gpost_ref, gpre_ref, x1_ref, xn_ref):
    m = mix_ref[...]
    x1 = x_ref[...] + m * lax.rsqrt(jnp.mean(m * m, axis=-1, keepdims=True) + RMS_EPS) * gpost_ref[...]
    x1_ref[...] = x1
    xn_ref[...] = (x1 * lax.rsqrt(jnp.mean(x1 * x1, axis=-1, keepdims=True) + RMS_EPS)
                   * gpre_ref[...]).astype(xn_ref.dtype)


def post_mix(mix, x, g_post, g_pre, tm):
    n, d = x.shape
    blk = pl.BlockSpec((tm, d), lambda i: (i, 0))
    vec = pl.BlockSpec((1, d), lambda i: (0, 0))
    return pl.pallas_call(
        _post_mix_kernel,
        grid=(n // tm,),
        in_specs=[blk, blk, vec, vec],
        out_specs=[blk, blk],
        out_shape=[jax.ShapeDtypeStruct((n, d), F32), jax.ShapeDtypeStruct((n, d), BF16)],
        compiler_params=_cparams("parallel"),
        name="post_mix",
    )(mix, x, g_post.reshape(1, d), g_pre.reshape(1, d))


def _ffn_kernel(xn_ref, w1_ref, w2_ref, o_ref):
    @pl.when(pl.program_id(1) == 0)
    def _():
        o_ref[...] = jnp.zeros_like(o_ref)

    h = jnp.maximum(_mm(xn_ref[...], w1_ref[...]), 0.0)
    o_ref[...] += _mm((h * h).astype(BF16), w2_ref[...])


def ffn(xn, w1, w2, tm, tf):
    n, d = xn.shape
    f = w1.shape[1]
    return pl.pallas_call(
        _ffn_kernel,
        grid=(n // tm, f // tf),
        in_specs=[pl.BlockSpec((tm, d), lambda i, j: (i, 0), pipeline_mode=pl.Buffered(1)),
                  pl.BlockSpec((d, tf), lambda i, j: (0, j)),
                  pl.BlockSpec((tf, d), lambda i, j: (j, 0))],
        out_specs=pl.BlockSpec((tm, d), lambda i, j: (i, 0)),
        out_shape=jax.ShapeDtypeStruct((n, d), F32),
        compiler_params=_cparams("parallel", "arbitrary"),
        name="ffn",
    )(xn, w1, w2)


def _post_ffn_kernel(f_ref, x_ref, g_ref, o_ref):
    f = f_ref[...]
    o_ref[...] = x_ref[...] + f * lax.rsqrt(jnp.mean(f * f, axis=-1, keepdims=True) + RMS_EPS) * g_ref[...]


def post_ffn(f, x1, g, tm):
    n, d = x1.shape
    blk = pl.BlockSpec((tm, d), lambda i: (i, 0))
    return pl.pallas_call(
        _post_ffn_kernel,
        grid=(n // tm,),
        in_specs=[blk, blk, pl.BlockSpec((1, d), lambda i: (0, 0))],
        out_specs=blk,
        out_shape=jax.ShapeDtypeStruct((n, d), F32),
        compiler_params=_cparams("parallel"),
        name="post_ffn",
    )(f, x1, g.reshape(1, d))


def _tile(n, want):
    if n <= want:
        return n
    t = want - want % V7X_LANES
    while n % t:
        t -= V7X_LANES
    assert t > 0, (n, want)
    return t


def kernel(x, pre_mix_g, w_in, tshift_mu, gmlp_ln_g, gmlp_ln_b, gmlp_ws, gmlp_bs, decay_w0, decay_up, iclr_a0, iclr_up, gate_up, k_k, k_a, r_k, lnx_g, lnx_b, w_out, post_mix_g, pre_ffn_g, w_ff1, w_ff2, post_ffn_g):
    batch, seq, d = x.shape
    depth = w_in.shape[0]
    n = batch * seq
    gw = gmlp_ws.shape[1] * V7X_LANES
    rw = decay_w0.shape[1]
    xf = x.reshape(n, d)
    for l in range(depth):
        h = rms_norm_bf16(xf, pre_mix_g[l], _tile(n, 512))
        p = matmul(h, w_in[l], _tile(n, 1024), 768, "w_in")

        y_a = gmlp_gating(p, gmlp_ln_g[l], gmlp_ln_b[l], gmlp_ws[l], gmlp_bs[l], _tile(seq, 512))
        y_b = rwkv_time_mix(p, tshift_mu[l], decay_w0[l], decay_up[l], iclr_a0[l], iclr_up[l],
                            gate_up[l], k_k[l], k_a[l], r_k[l].reshape(-1), lnx_g[l], lnx_b[l],
                            batch, seq, _tile(seq, 1024), first_col=2 * gw)

        mix = matmul_concat2(y_a, y_b, w_out[l].astype(BF16), _tile(n, 1024), _tile(d, 1024), "w_out")
        x1, xn = post_mix(mix, xf, post_mix_g[l], pre_ffn_g[l], _tile(n, 256))
        f = ffn(xn, w_ff1[l].astype(BF16), w_ff2[l].astype(BF16), _tile(n, 1024), _tile(w_ff1.shape[2], 512))
        xf = post_ffn(f, x1, post_ffn_g[l], _tile(n, 256))
    return xf.reshape(batch, seq, d)
```

```python
import functools
import math

import jax
import jax.numpy as jnp
from jax import lax
from jax.experimental import pallas as pl
from jax.experimental.pallas import tpu as pltpu

F32 = jnp.float32
BF16 = jnp.bfloat16

RMS_EPS = 1e-6
LN_EPS = 1e-5
L2_EPS = 1e-12
GN_EPS_PER_CH = 1e-5
DECAY_LOG_SCALE = math.exp(-0.5)

V7X_LANES = 128
V7X_SUBLANES = 8
RWKV_HEAD_DIM = 64
GMLP_BLOCK = 128
STREAM_CHUNK = 64
SCAN_CHUNK = 64
SCAN_HEADS = 4
VMEM_LIMIT = 60 * 1024 * 1024


def _cparams(*sem):
    return pltpu.CompilerParams(dimension_semantics=sem, vmem_limit_bytes=VMEM_LIMIT)


def _nt(a, b):
    return lax.dot_general(a, b, (((1,), (1,)), ((), ())), preferred_element_type=F32)


def _mm(a, b, precision=None):
    return jnp.dot(a, b, precision=precision, preferred_element_type=F32)


def _mmb(a, b):
    return _mm(a.astype(BF16), b.astype(BF16))


def _rms_kernel(x_ref, g_ref, o_ref):
    x = x_ref[...]
    ms = jnp.mean(x * x, axis=-1, keepdims=True)
    o_ref[...] = (x * lax.rsqrt(ms + RMS_EPS) * g_ref[...]).astype(o_ref.dtype)


def rms_norm_bf16(x, g, tm):
    n, d = x.shape
    return pl.pallas_call(
        _rms_kernel,
        grid=(n // tm,),
        in_specs=[pl.BlockSpec((tm, d), lambda i: (i, 0)), pl.BlockSpec((1, d), lambda i: (0, 0))],
        out_specs=pl.BlockSpec((tm, d), lambda i: (i, 0)),
        out_shape=jax.ShapeDtypeStruct((n, d), BF16),
        compiler_params=_cparams("parallel"),
        name="rms_norm",
    )(x, g.reshape(1, d))


def _mm_kernel(a_ref, w_ref, o_ref):
    o_ref[...] = _mm(a_ref[...], w_ref[...].astype(a_ref.dtype)).astype(o_ref.dtype)


def matmul(a, w, tm, tn, name):
    m, k = a.shape
    n = w.shape[1]
    return pl.pallas_call(
        _mm_kernel,
        grid=(m // tm, pl.cdiv(n, tn)),
        in_specs=[pl.BlockSpec((tm, k), lambda i, j: (i, 0)), pl.BlockSpec((k, tn), lambda i, j: (0, j))],
        out_specs=pl.BlockSpec((tm, tn), lambda i, j: (i, j)),
        out_shape=jax.ShapeDtypeStruct((m, n), F32),
        compiler_params=_cparams("parallel", "arbitrary"),
        name=name,
    )(a, w)


def _mm2_kernel(a1_ref, a2_ref, w1_ref, w2_ref, o_ref):
    o_ref[...] = _mm(a1_ref[...], w1_ref[...]) + _mm(a2_ref[...], w2_ref[...])


def matmul_concat2(a1, a2, w, tm, tn, name):
    m, k1 = a1.shape
    k2 = a2.shape[1]
    n = w.shape[1]
    assert k1 == k2
    return pl.pallas_call(
        _mm2_kernel,
        grid=(m // tm, n // tn),
        in_specs=[pl.BlockSpec((tm, k1), lambda i, j: (i, 0)),
                  pl.BlockSpec((tm, k2), lambda i, j: (i, 0)),
                  pl.BlockSpec((k1, tn), lambda i, j: (0, j)),
                  pl.BlockSpec((k2, tn), lambda i, j: (1, j))],
        out_specs=pl.BlockSpec((tm, tn), lambda i, j: (i, j)),
        out_shape=jax.ShapeDtypeStruct((m, n), F32),
        compiler_params=_cparams("parallel", "arbitrary"),
        name=name,
    )(a1, a2, w, w)


def _gmlp_kernel(u_ref, v_ref, lng_ref, lnb_ref, ws_ref, bs_ref, o_ref, *, heads, nblk):
    zv = jax.nn.gelu(v_ref[...])
    mean = jnp.mean(zv, axis=-1, keepdims=True)
    xc = zv - mean
    var = jnp.mean(xc * xc, axis=-1, keepdims=True)
    vn = (xc * lax.rsqrt(var + LN_EPS) * lng_ref[...] + lnb_ref[...]).astype(BF16)
    row = lax.broadcasted_iota(jnp.int32, (GMLP_BLOCK, GMLP_BLOCK), 0) // STREAM_CHUNK
    col = lax.broadcasted_iota(jnp.int32, (GMLP_BLOCK, GMLP_BLOCK), 1) // STREAM_CHUNK
    causal = col <= row
    for h in range(heads):
        wm = jnp.where(causal, ws_ref[h], 0.0).astype(BF16)
        bias = bs_ref[h]
        cs = slice(h * V7X_LANES, (h + 1) * V7X_LANES)
        for n in range(nblk):
            rs = slice(n * GMLP_BLOCK, (n + 1) * GMLP_BLOCK)
            mixed = _mm(wm, vn[rs, cs]) + bias
            o_ref[rs, cs] = (jax.nn.gelu(u_ref[rs, cs]) * mixed).astype(o_ref.dtype)


def gmlp_gating(p_main, ln_g, ln_b, ws, bs, rows):
    n = p_main.shape[0]
    heads = ws.shape[0]
    gw = heads * V7X_LANES
    assert ws.shape[1:] == (GMLP_BLOCK, GMLP_BLOCK) and rows % GMLP_BLOCK == 0
    kern = functools.partial(_gmlp_kernel, heads=heads, nblk=rows // GMLP_BLOCK)
    return pl.pallas_call(
        kern,
        grid=(n // rows,),
        in_specs=[pl.BlockSpec((rows, gw), lambda i: (i, 0)),
                  pl.BlockSpec((rows, gw), lambda i: (i, 1)),
                  pl.BlockSpec((1, gw), lambda i: (0, 0)),
                  pl.BlockSpec((1, gw), lambda i: (0, 0)),
                  pl.BlockSpec((heads, GMLP_BLOCK, GMLP_BLOCK), lambda i: (0, 0, 0)),
                  pl.BlockSpec((heads, GMLP_BLOCK, 1), lambda i: (0, 0, 0))],
        out_specs=pl.BlockSpec((rows, gw), lambda i: (i, 0)),
        out_shape=jax.ShapeDtypeStruct((n, gw), BF16),
        compiler_params=_cparams("parallel"),
        name="gmlp_gating",
    )(p_main, p_main, ln_g.reshape(1, gw), ln_b.reshape(1, gw), ws, bs[:, :, None])


def _rwkv_kernel(r_ref, k_ref, v_ref, t0_ref, t1_ref, t2_ref,
                 mur_ref, muk_ref, muv_ref, mut_ref, w0_ref, wup_ref, a0_ref, aup_ref, gup_ref,
                 kk_ref, ka_ref, rk_ref, lng_ref, lnb_ref, o_ref, state_ref, carry_ref, tcarry_ref,
                 *, heads, nchunks):
    C = SCAN_CHUNK
    N = RWKV_HEAD_DIM
    L = heads * N
    SUB = V7X_SUBLANES

    @pl.when(pl.program_id(2) == 0)
    def _():
        state_ref[...] = jnp.zeros_like(state_ref)
        carry_ref[...] = jnp.zeros_like(carry_ref)
        tcarry_ref[...] = jnp.zeros_like(tcarry_ref)

    def shifted(ref, carry, i, mu):
        p = ref[...]
        before = carry[i][SUB - 1:SUB, :]
        carry[i] = ref[p.shape[0] - SUB:, :]
        rows = lax.broadcasted_iota(jnp.int32, p.shape, 0)
        prev = jnp.where(rows == 0, before, pltpu.roll(p, 1, 0))
        return p + (prev - p) * mu

    r_all = shifted(r_ref, carry_ref, 0, mur_ref[...])
    k_all = shifted(k_ref, carry_ref, 1, muk_ref[...])
    v_all = shifted(v_ref, carry_ref, 2, muv_ref[...])
    mut = mut_ref[...]
    W = V7X_LANES
    lora_in = shifted(t0_ref, tcarry_ref, 0, mut[:, :W])
    xg = jnp.concatenate([shifted(t1_ref, tcarry_ref, 1, mut[:, W:2 * W]),
                          shifted(t2_ref, tcarry_ref, 2, mut[:, 2 * W:])], axis=1)

    zw = w0_ref[...] + _mm(jnp.tanh(lora_in).astype(BF16), wup_ref[...])
    lw_all = jax.nn.sigmoid(zw) * (-DECAY_LOG_SCALE)
    a_all = jax.nn.sigmoid(a0_ref[...] + _mm(lora_in.astype(BF16), aup_ref[...]))
    g_all = _mm(jax.nn.sigmoid(xg).astype(BF16), gup_ref[...])

    r2 = lax.broadcasted_iota(jnp.int32, (L, L), 0)
    c2 = lax.broadcasted_iota(jnp.int32, (L, L), 1)
    bd_mask = (r2 // N) == (c2 // N)
    eye2 = r2 == c2
    ones_bd = bd_mask.astype(BF16)

    def head_sum(x):
        return _mm(x.astype(BF16), ones_bd)

    kk_all = k_all * kk_ref[...]
    kk_all = kk_all / jnp.maximum(jnp.sqrt(head_sum(kk_all * kk_all)), L2_EPS)
    k2_all = k_all * (1.0 + (a_all - 1.0) * ka_ref[...])
    bv_all = head_sum(r_all * k2_all * rk_ref[...]) * v_all
    beta_all = kk_all * a_all

    ti = lax.broadcasted_iota(jnp.int32, (C, C), 0)
    si = lax.broadcasted_iota(jnp.int32, (C, C), 1)
    tri = (ti >= si).astype(BF16)
    lane = lax.broadcasted_iota(jnp.int32, (C, L), 1)
    rowi = lax.broadcasted_iota(jnp.int32, (C, L), 0)
    scol = lane % N
    strict = rowi > scol
    incl = rowi >= scol
    eye_p = (rowi == scol).astype(F32)
    head_sel = [lane // N == h for h in range(heads)]

    def sm(x):
        xb = x.astype(BF16)
        zero = jnp.zeros_like(xb)
        return jnp.concatenate([jnp.where(sel, xb, zero) for sel in head_sel], axis=0)

    chunks = range(nchunks)

    def each(fn, *lists):
        return [fn(*xs) for xs in zip(*lists)]

    rows = [slice(c * C, (c + 1) * C) for c in chunks]
    lw = [lw_all[rs] for rs in rows]
    def cumsum_rows(x):
        h1 = x.astype(BF16)
        r1 = x - h1.astype(F32)
        h2 = r1.astype(BF16)
        h3 = (r1 - h2.astype(F32)).astype(BF16)
        return _mm(tri, h1) + _mm(tri, h2) + _mm(tri, h3)

    cm = [cumsum_rows(x) for x in lw]
    r = [r_all[rs] for rs in rows]
    k = [k2_all[rs] for rs in rows]
    v = [v_all[rs] for rs in rows]
    kk = [kk_all[rs] for rs in rows]
    beta = [beta_all[rs] for rs in rows]
    ginv = each(lambda m: jnp.exp(-m), cm)
    a_t = each(lambda kk_, m, w: -kk_ * jnp.exp(m - w), kk, cm, lw)
    b_t = each(jnp.multiply, beta, ginv)
    k_t = each(jnp.multiply, k, ginv)
    r_t = each(lambda r_, m: r_ * jnp.exp(m), r, cm)
    dec_end = each(lambda m: jnp.exp(jnp.broadcast_to(m[C - 1:C, :], (C, L)) - m), cm)
    b_h = each(jnp.multiply, beta, dec_end)
    k_h = each(jnp.multiply, k, dec_end)

    sc = each(lambda a_, r_, b_, k_: _nt(jnp.concatenate([a_, r_], axis=0).astype(BF16),
                                         jnp.concatenate([sm(b_), sm(k_)], axis=0)),
              a_t, r_t, b_t, k_t)
    ab = [x[:C, :L] for x in sc]
    l_ak = [jnp.where(strict, x[:C, L:], 0.0) for x in sc]
    p_rb = [jnp.where(incl, x[C:, :L], 0.0) for x in sc]
    p_rk = [jnp.where(incl, x[C:, L:], 0.0) for x in sc]

    base = 8
    d = [jnp.where(strict & (rowi // base == scol // base), x, 0.0) for x in ab]
    t_inv = [eye_p + x for x in d]
    p2 = each(lambda x: _mmb(x, sm(x)), d)
    both = each(lambda p, t: _mmb(p, jnp.concatenate([sm(p), sm(t)], axis=1)), p2, t_inv)
    t_inv = each(lambda t, bo: t + bo[:, L:], t_inv, both)
    t_inv = each(lambda t, bo: t + _mmb(bo[:, :L], sm(t)), t_inv, both)
    b = base
    while b < C:
        blk_off = (rowi // (2 * b) == scol // (2 * b)) & (rowi // b > scol // b)
        ot = each(lambda x, t: _mmb(jnp.where(blk_off, x, 0.0), sm(t)), ab, t_inv)
        t_inv = each(lambda t, o: t + _mmb(t, sm(o)), t_inv, ot)
        b *= 2

    sm_v = each(sm, v)
    x1 = each(_mmb, l_ak, sm_v)
    wv = each(lambda t, a_, x_: _mmb(t, jnp.concatenate([sm(a_), sm(x_)], axis=1)), t_inv, a_t, x1)
    w_t = [x[:, :L] for x in wv]
    v_t = [x[:, L:] for x in wv]
    q = each(lambda r_, p, w: (r_ + _mmb(p, sm(w))).astype(BF16), r_t, p_rb, w_t)
    y0 = each(lambda pb, pk, vt, sv: _mmb(jnp.concatenate([pb, pk], axis=1), jnp.concatenate([sm(vt), sv], axis=0)),
              p_rb, p_rk, v_t, sm_v)
    b_ht = [x.T for x in b_h]
    k_ht = [x.T for x in k_h]
    m_bd = each(lambda bt, w: jnp.where(bd_mask, _mmb(bt, w), 0.0).astype(BF16), b_ht, w_t)
    n_bd = each(lambda bt, kt, vt, v_: jnp.where(bd_mask, _mmb(jnp.concatenate([bt, kt], axis=1),
                                                                jnp.concatenate([vt, v_], axis=0)), 0.0),
                b_ht, k_ht, v_t, v)
    gcol = each(lambda m: jnp.sum(jnp.where(eye2, jnp.exp(jnp.broadcast_to(m[C - 1:C, :], (L, L))), 0.0),
                                  axis=1, keepdims=True), cm)

    S = state_ref[...]
    ys = []
    for c in chunks:
        sb = S.astype(BF16)
        ys.append(_mm(q[c], sb) + y0[c])
        S = gcol[c] * S + _mm(m_bd[c], sb) + n_bd[c]
    state_ref[...] = S

    y = jnp.concatenate(ys, axis=0)
    inv_n = 1.0 / N
    yc = y - head_sum(y) * inv_n
    var = head_sum(yc * yc) * inv_n
    yn = yc * lax.rsqrt(var + N * GN_EPS_PER_CH) * lng_ref[...] + lnb_ref[...]
    o_ref[...] = ((yn + bv_all) * g_all).astype(o_ref.dtype)


def rwkv_time_mix(p, mu, w0, w_up, a0, a_up, g_up, k_k, k_a, r_k, lnx_g, lnx_b,
                  batch, seq, tc, first_col):
    n = p.shape[0]
    rw = w0.shape[0]
    L = SCAN_HEADS * RWKV_HEAD_DIM
    dl, il = w_up.shape[0], a_up.shape[0]
    W = V7X_LANES
    tw = p.shape[1] - first_col - 3 * rw
    tb = (first_col + 3 * rw) // W
    assert tw == 3 * W and g_up.shape[0] == 2 * W and (first_col + 3 * rw) % W == 0
    assert dl + il == V7X_LANES and seq % tc == 0 and tc % SCAN_CHUNK == 0 and rw % L == 0 and first_col % L == 0
    nt = seq // tc
    ng = rw // L
    cb = first_col // L
    sub = V7X_SUBLANES
    row = lambda a: a.reshape(1, -1)
    mur, muk, muv, mut = mu[:rw], mu[rw:2 * rw], mu[2 * rw:3 * rw], mu[3 * rw:]
    wup_pad = jnp.concatenate([w_up, jnp.zeros((il, rw), w_up.dtype)], axis=0).astype(BF16)
    aup_pad = jnp.concatenate([jnp.zeros((dl, rw), a_up.dtype), a_up], axis=0).astype(BF16)
    gup = g_up.astype(BF16)

    blk = lambda i: pl.BlockSpec((tc, L), lambda b, g, c: (b * nt + c, cb + i * ng + g))
    tail = lambda j: pl.BlockSpec((tc, W), lambda b, g, c: (b * nt + c, tb + j))
    vec = pl.BlockSpec((1, L), lambda b, g, c: (0, g))
    lora = lambda a: pl.BlockSpec((a.shape[0], L), lambda b, g, c: (0, g))
    return pl.pallas_call(
        functools.partial(_rwkv_kernel, heads=SCAN_HEADS, nchunks=tc // SCAN_CHUNK),
        grid=(batch, ng, nt),
        in_specs=[blk(0), blk(1), blk(2), tail(0), tail(1), tail(2),
                  vec, vec, vec, pl.BlockSpec((1, tw), lambda b, g, c: (0, 0)),
                  vec, lora(wup_pad), vec, lora(aup_pad), lora(gup), vec, vec, vec, vec, vec],
        out_specs=pl.BlockSpec((tc, L), lambda b, g, c: (b * nt + c, g)),
        out_shape=jax.ShapeDtypeStruct((n, rw), BF16),
        scratch_shapes=[pltpu.VMEM((L, L), F32),
                        pltpu.VMEM((3, sub, L), F32),
                        pltpu.VMEM((3, sub, W), F32)],
        compiler_params=_cparams("parallel", "parallel", "arbitrary"),
        name="rwkv_time_mix",
    )(*([p] * 6),
      row(mur), row(muk), row(muv), row(mut), row(w0), wup_pad, row(a0), aup_pad, gup,
      row(k_k), row(k_a), row(r_k), row(lnx_g), row(lnx_b))


def _post_mix_kernel(mix_ref, x_ref, gpost_ref, gpre_ref, x1_ref, xn_ref):
    m = mix_ref[...]
    x1 = x_ref[...] + m * lax.rsqrt(jnp.mean(m * m, axis=-1, keepdims=True) + RMS_EPS) * gpost_ref[...]
    x1_ref[...] = x1
    xn_ref[...] = (x1 * lax.rsqrt(jnp.mean(x1 * x1, axis=-1, keepdims=True) + RMS_EPS)
                   * gpre_ref[...]).astype(xn_ref.dtype)


def post_mix(mix, x, g_post, g_pre, tm):
    n, d = x.shape
    blk = pl.BlockSpec((tm, d), lambda i: (i, 0))
    vec = pl.BlockSpec((1, d), lambda i: (0, 0))
    return pl.pallas_call(
        _post_mix_kernel,
        grid=(n // tm,),
        in_specs=[blk, blk, vec, vec],
        out_specs=[blk, blk],
        out_shape=[jax.ShapeDtypeStruct((n, d), F32), jax.ShapeDtypeStruct((n, d), BF16)],
        compiler_params=_cparams("parallel"),
        name="post_mix",
    )(mix, x, g_post.reshape(1, d), g_pre.reshape(1, d))


def _ffn_kernel(xn_ref, w1_ref, w2_ref, o_ref):
    @pl.when(pl.program_id(1) == 0)
    def _():
        o_ref[...] = jnp.zeros_like(o_ref)

    h = jnp.maximum(_mm(xn_ref[...], w1_ref[...]), 0.0)
    o_ref[...] += _mm((h * h).astype(BF16), w2_ref[...])


def ffn(xn, w1, w2, tm, tf):
    n, d = xn.shape
    f = w1.shape[1]
    return pl.pallas_call(
        _ffn_kernel,
        grid=(n // tm, f // tf),
        in_specs=[pl.BlockSpec((tm, d), lambda i, j: (i, 0), pipeline_mode=pl.Buffered(1)),
                  pl.BlockSpec((d, tf), lambda i, j: (0, j)),
                  pl.BlockSpec((tf, d), lambda i, j: (j, 0))],
        out_specs=pl.BlockSpec((tm, d), lambda i, j: (i, 0)),
        out_shape=jax.ShapeDtypeStruct((n, d), F32),
        compiler_params=_cparams("parallel", "arbitrary"),
        name="ffn",
    )(xn, w1, w2)


def _post_ffn_kernel(f_ref, x_ref, g_ref, o_ref):
    f = f_ref[...]
    o_ref[...] = x_ref[...] + f * lax.rsqrt(jnp.mean(f * f, axis=-1, keepdims=True) + RMS_EPS) * g_ref[...]


def post_ffn(f, x1, g, tm):
    n, d = x1.shape
    blk = pl.BlockSpec((tm, d), lambda i: (i, 0))
    return pl.pallas_call(
        _post_ffn_kernel,
        grid=(n // tm,),
        in_specs=[blk, blk, pl.BlockSpec((1, d), lambda i: (0, 0))],
        out_specs=blk,
        out_shape=jax.ShapeDtypeStruct((n, d), F32),
        compiler_params=_cparams("parallel"),
        name="post_ffn",
    )(f, x1, g.reshape(1, d))


def _tile(n, want):
    if n <= want:
        return n
    t = want - want % V7X_LANES
    while n % t:
        t -= V7X_LANES
    assert t > 0, (n, want)
    return t


def kernel(x, pre_mix_g, w_in, tshift_mu, gmlp_ln_g, gmlp_ln_b, gmlp_ws, gmlp_bs, decay_w0, decay_up, iclr_a0, iclr_up, gate_up, k_k, k_a, r_k, lnx_g, lnx_b, w_out, post_mix_g, pre_ffn_g, w_ff1, w_ff2, post_ffn_g):
    batch, seq, d = x.shape
    depth = w_in.shape[0]
    n = batch * seq
    gw = gmlp_ws.shape[1] * V7X_LANES
    rw = decay_w0.shape[1]
    xf = x.reshape(n, d)
    for l in range(depth):
        h = rms_norm_bf16(xf, pre_mix_g[l], _tile(n, 512))
        p = matmul(h, w_in[l], _tile(n, 1024), 768, "w_in")

        y_a = gmlp_gating(p, gmlp_ln_g[l], gmlp_ln_b[l], gmlp_ws[l], gmlp_bs[l], _tile(seq, 512))
        y_b = rwkv_time_mix(p, tshift_mu[l], decay_w0[l], decay_up[l], iclr_a0[l], iclr_up[l],
                            gate_up[l], k_k[l], k_a[l], r_k[l].reshape(-1), lnx_g[l], lnx_b[l],
                            batch, seq, _tile(seq, 512), first_col=2 * gw)

        mix = matmul_concat2(y_a, y_b, w_out[l].astype(BF16), _tile(n, 1024), _tile(d, 1024), "w_out")
        x1, xn = post_mix(mix, xf, post_mix_g[l], pre_ffn_g[l], _tile(n, 256))
        f = ffn(xn, w_ff1[l].astype(BF16), w_ff2[l].astype(BF16), _tile(n, 1024), _tile(w_ff1.shape[2], 512))
        xf = post_ffn(f, x1, post_ffn_g[l], _tile(n, 256))
    return xf.reshape(batch, seq, d)
```

```python
import functools
import math

import jax
import jax.numpy as jnp
from jax import lax
from jax.experimental import pallas as pl
from jax.experimental.pallas import tpu as pltpu

F32 = jnp.float32
BF16 = jnp.bfloat16

RMS_EPS = 1e-6
LN_EPS = 1e-5
L2_EPS = 1e-12
GN_EPS_PER_CH = 1e-5
DECAY_LOG_SCALE = math.exp(-0.5)

V7X_LANES = 128
V7X_SUBLANES = 8
RWKV_HEAD_DIM = 64
GMLP_BLOCK = 128
STREAM_CHUNK = 64
SCAN_CHUNK = 64
SCAN_HEADS = 4
VMEM_LIMIT = 60 * 1024 * 1024


def _cparams(*sem):
    return pltpu.CompilerParams(dimension_semantics=sem, vmem_limit_bytes=VMEM_LIMIT)


def _nt(a, b):
    return lax.dot_general(a, b, (((1,), (1,)), ((), ())), preferred_element_type=F32)


def _mm(a, b, precision=None):
    return jnp.dot(a, b, precision=precision, preferred_element_type=F32)


def _mmb(a, b):
    return _mm(a.astype(BF16), b.astype(BF16))


def _rms_kernel(x_ref, g_ref, o_ref):
    x = x_ref[...]
    ms = jnp.mean(x * x, axis=-1, keepdims=True)
    o_ref[...] = (x * lax.rsqrt(ms + RMS_EPS) * g_ref[...]).astype(o_ref.dtype)


def rms_norm_bf16(x, g, tm):
    n, d = x.shape
    return pl.pallas_call(
        _rms_kernel,
        grid=(n // tm,),
        in_specs=[pl.BlockSpec((tm, d), lambda i: (i, 0)), pl.BlockSpec((1, d), lambda i: (0, 0))],
        out_specs=pl.BlockSpec((tm, d), lambda i: (i, 0)),
        out_shape=jax.ShapeDtypeStruct((n, d), BF16),
        compiler_params=_cparams("parallel"),
        name="rms_norm",
    )(x, g.reshape(1, d))


def _mm_kernel(a_ref, w_ref, o_ref):
    o_ref[...] = _mm(a_ref[...], w_ref[...].astype(a_ref.dtype)).astype(o_ref.dtype)


def matmul(a, w, tm, tn, name):
    m, k = a.shape
    n = w.shape[1]
    return pl.pallas_call(
        _mm_kernel,
        grid=(m // tm, pl.cdiv(n, tn)),
        in_specs=[pl.BlockSpec((tm, k), lambda i, j: (i, 0)), pl.BlockSpec((k, tn), lambda i, j: (0, j))],
        out_specs=pl.BlockSpec((tm, tn), lambda i, j: (i, j)),
        out_shape=jax.ShapeDtypeStruct((m, n), F32),
        compiler_params=_cparams("parallel", "arbitrary"),
        name=name,
    )(a, w)


def _mm2_kernel(a1_ref, a2_ref, w1_ref, w2_ref, o_ref):
    o_ref[...] = _mm(a1_ref[...], w1_ref[...]) + _mm(a2_ref[...], w2_ref[...])


def matmul_concat2(a1, a2, w, tm, tn, name):
    m, k1 = a1.shape
    k2 = a2.shape[1]
    n = w.shape[1]
    assert k1 == k2
    return pl.pallas_call(
        _mm2_kernel,
        grid=(m // tm, n // tn),
        in_specs=[pl.BlockSpec((tm, k1), lambda i, j: (i, 0)),
                  pl.BlockSpec((tm, k2), lambda i, j: (i, 0)),
                  pl.BlockSpec((k1, tn), lambda i, j: (0, j)),
                  pl.BlockSpec((k2, tn), lambda i, j: (1, j))],
        out_specs=pl.BlockSpec((tm, tn), lambda i, j: (i, j)),
        out_shape=jax.ShapeDtypeStruct((m, n), F32),
        compiler_params=_cparams("parallel", "arbitrary"),
        name=name,
    )(a1, a2, w, w)


def _gmlp_kernel(u_ref, v_ref, lng_ref, lnb_ref, ws_ref, bs_ref, o_ref, *, heads, nblk):
    zv = jax.nn.gelu(v_ref[...])
    mean = jnp.mean(zv, axis=-1, keepdims=True)
    xc = zv - mean
    var = jnp.mean(xc * xc, axis=-1, keepdims=True)
    vn = (xc * lax.rsqrt(var + LN_EPS) * lng_ref[...] + lnb_ref[...]).astype(BF16)
    row = lax.broadcasted_iota(jnp.int32, (GMLP_BLOCK, GMLP_BLOCK), 0) // STREAM_CHUNK
    col = lax.broadcasted_iota(jnp.int32, (GMLP_BLOCK, GMLP_BLOCK), 1) // STREAM_CHUNK
    causal = col <= row
    for h in range(heads):
        wm = jnp.where(causal, ws_ref[h], 0.0).astype(BF16)
        bias = bs_ref[h]
        cs = slice(h * V7X_LANES, (h + 1) * V7X_LANES)
        for n in range(nblk):
            rs = slice(n * GMLP_BLOCK, (n + 1) * GMLP_BLOCK)
            mixed = _mm(wm, vn[rs, cs]) + bias
            o_ref[rs, cs] = (jax.nn.gelu(u_ref[rs, cs]) * mixed).astype(o_ref.dtype)


def gmlp_gating(p_main, ln_g, ln_b, ws, bs, rows):
    n = p_main.shape[0]
    heads = ws.shape[0]
    gw = heads * V7X_LANES
    assert ws.shape[1:] == (GMLP_BLOCK, GMLP_BLOCK) and rows % GMLP_BLOCK == 0
    kern = functools.partial(_gmlp_kernel, heads=heads, nblk=rows // GMLP_BLOCK)
    return pl.pallas_call(
        kern,
        grid=(n // rows,),
        in_specs=[pl.BlockSpec((rows, gw), lambda i: (i, 0)),
                  pl.BlockSpec((rows, gw), lambda i: (i, 1)),
                  pl.BlockSpec((1, gw), lambda i: (0, 0)),
                  pl.BlockSpec((1, gw), lambda i: (0, 0)),
                  pl.BlockSpec((heads, GMLP_BLOCK, GMLP_BLOCK), lambda i: (0, 0, 0)),
                  pl.BlockSpec((heads, GMLP_BLOCK, 1), lambda i: (0, 0, 0))],
        out_specs=pl.BlockSpec((rows, gw), lambda i: (i, 0)),
        out_shape=jax.ShapeDtypeStruct((n, gw), BF16),
        compiler_params=_cparams("parallel"),
        name="gmlp_gating",
    )(p_main, p_main, ln_g.reshape(1, gw), ln_b.reshape(1, gw), ws, bs[:, :, None])


def _rwkv_kernel(r_ref, k_ref, v_ref, t0_ref, t1_ref, t2_ref,
                 mur_ref, muk_ref, muv_ref, mut_ref, w0_ref, wup_ref, a0_ref, aup_ref, gup_ref,
                 kk_ref, ka_ref, rk_ref, lng_ref, lnb_ref, o_ref, state_ref, carry_ref, tcarry_ref,
                 *, heads, nchunks):
    C = SCAN_CHUNK
    N = RWKV_HEAD_DIM
    L = heads * N
    SUB = V7X_SUBLANES

    @pl.when(pl.program_id(2) == 0)
    def _():
        state_ref[...] = jnp.zeros_like(state_ref)
        carry_ref[...] = jnp.zeros_like(carry_ref)
        tcarry_ref[...] = jnp.zeros_like(tcarry_ref)

    def shifted(ref, carry, i, mu):
        p = ref[...]
        before = carry[i][SUB - 1:SUB, :]
        carry[i] = ref[p.shape[0] - SUB:, :]
        rows = lax.broadcasted_iota(jnp.int32, p.shape, 0)
        prev = jnp.where(rows == 0, before, pltpu.roll(p, 1, 0))
        return p + (prev - p) * mu

    r_all = shifted(r_ref, carry_ref, 0, mur_ref[...])
    k_all = shifted(k_ref, carry_ref, 1, muk_ref[...])
    v_all = shifted(v_ref, carry_ref, 2, muv_ref[...])
    mut = mut_ref[...]
    W = V7X_LANES
    lora_in = shifted(t0_ref, tcarry_ref, 0, mut[:, :W])
    xg = jnp.concatenate([shifted(t1_ref, tcarry_ref, 1, mut[:, W:2 * W]),
                          shifted(t2_ref, tcarry_ref, 2, mut[:, 2 * W:])], axis=1)

    zw = w0_ref[...] + _mm(jnp.tanh(lora_in).astype(BF16), wup_ref[...])
    lw_all = jax.nn.sigmoid(zw) * (-DECAY_LOG_SCALE)
    a_all = jax.nn.sigmoid(a0_ref[...] + _mm(lora_in.astype(BF16), aup_ref[...]))
    g_all = _mm(jax.nn.sigmoid(xg).astype(BF16), gup_ref[...])

    r2 = lax.broadcasted_iota(jnp.int32, (L, L), 0)
    c2 = lax.broadcasted_iota(jnp.int32, (L, L), 1)
    bd_mask = (r2 // N) == (c2 // N)
    eye2 = r2 == c2
    ones_bd = bd_mask.astype(BF16)

    def head_sum(x):
        return _mm(x.astype(BF16), ones_bd)

    kk_all = k_all * kk_ref[...]
    kk_all = kk_all / jnp.maximum(jnp.sqrt(head_sum(kk_all * kk_all)), L2_EPS)
    k2_all = k_all * (1.0 + (a_all - 1.0) * ka_ref[...])
    bv_all = head_sum(r_all * k2_all * rk_ref[...]) * v_all
    beta_all = kk_all * a_all

    ti = lax.broadcasted_iota(jnp.int32, (C, C), 0)
    si = lax.broadcasted_iota(jnp.int32, (C, C), 1)
    tri = (ti >= si).astype(BF16)
    lane = lax.broadcasted_iota(jnp.int32, (C, L), 1)
    rowi = lax.broadcasted_iota(jnp.int32, (C, L), 0)
    scol = lane % N
    strict = rowi > scol
    incl = rowi >= scol
    eye_p = (rowi == scol).astype(F32)
    head_sel = [lane // N == h for h in range(heads)]

    def sm(x):
        xb = x.astype(BF16)
        zero = jnp.zeros_like(xb)
        return jnp.concatenate([jnp.where(sel, xb, zero) for sel in head_sel], axis=0)

    chunks = range(nchunks)

    def each(fn, *lists):
        return [fn(*xs) for xs in zip(*lists)]

    rows = [slice(c * C, (c + 1) * C) for c in chunks]
    lw = [lw_all[rs] for rs in rows]
    def cumsum_rows(x):
        h1 = x.astype(BF16)
        r1 = x - h1.astype(F32)
        h2 = r1.astype(BF16)
        h3 = (r1 - h2.astype(F32)).astype(BF16)
        return _mm(tri, h1) + _mm(tri, h2) + _mm(tri, h3)

    cm = [cumsum_rows(x) for x in lw]
    r = [r_all[rs] for rs in rows]
    k = [k2_all[rs] for rs in rows]
    v = [v_all[rs] for rs in rows]
    kk = [kk_all[rs] for rs in rows]
    beta = [beta_all[rs] for rs in rows]
    ginv = each(lambda m: jnp.exp(-m), cm)
    a_t = each(lambda kk_, m, w: -kk_ * jnp.exp(m - w), kk, cm, lw)
    b_t = each(jnp.multiply, beta, ginv)
    k_t = each(jnp.multiply, k, ginv)
    r_t = each(lambda r_, m: r_ * jnp.exp(m), r, cm)
    dec_end = each(lambda m: jnp.exp(jnp.broadcast_to(m[C - 1:C, :], (C, L)) - m), cm)
    b_h = each(jnp.multiply, beta, dec_end)
    k_h = each(jnp.multiply, k, dec_end)

    sc = each(lambda a_, r_, b_, k_: _nt(jnp.concatenate([a_, r_], axis=0).astype(BF16),
                                         jnp.concatenate([sm(b_), sm(k_)], axis=0)),
              a_t, r_t, b_t, k_t)
    ab = [x[:C, :L] for x in sc]
    l_ak = [jnp.where(strict, x[:C, L:], 0.0) for x in sc]
    p_rb = [jnp.where(incl, x[C:, :L], 0.0) for x in sc]
    p_rk = [jnp.where(incl, x[C:, L:], 0.0) for x in sc]

    base = 8
    d = [jnp.where(strict & (rowi // base == scol // base), x, 0.0) for x in ab]
    t_inv = [eye_p + x for x in d]
    p2 = each(lambda x: _mmb(x, sm(x)), d)
    both = each(lambda p, t: _mmb(p, jnp.concatenate([sm(p), sm(t)], axis=1)), p2, t_inv)
    t_inv = each(lambda t, bo: t + bo[:, L:], t_inv, both)
    t_inv = each(lambda t, bo: t + _mmb(bo[:, :L], sm(t)), t_inv, both)
    b = base
    while b < C:
        blk_off = (rowi // (2 * b) == scol // (2 * b)) & (rowi // b > scol // b)
        ot = each(lambda x, t: _mmb(jnp.where(blk_off, x, 0.0), sm(t)), ab, t_inv)
        t_inv = each(lambda t, o: t + _mmb(t, sm(o)), t_inv, ot)
        b *= 2

    sm_v = each(sm, v)
    x1 = each(_mmb, l_ak, sm_v)
    wv = each(lambda t, a_, x_: _mmb(t, jnp.concatenate([sm(a_), sm(x_)], axis=1)), t_inv, a_t, x1)
    w_t = [x[:, :L] for x in wv]
    v_t = [x[:, L:] for x in wv]
    q = each(lambda r_, p, w: (r_ + _mmb(p, sm(w))).astype(BF16), r_t, p_rb, w_t)
    y0 = each(lambda pb, pk, vt, sv: _mmb(jnp.concatenate([pb, pk], axis=1), jnp.concatenate([sm(vt), sv], axis=0)),
              p_rb, p_rk, v_t, sm_v)
    b_ht = [x.T for x in b_h]
    k_ht = [x.T for x in k_h]
    def pack_diag(x):
        xm = jnp.where(bd_mask, x, 0.0)
        out = xm[:N]
        for h in range(1, heads):
            out = out + xm[h * N:(h + 1) * N]
        return out

    m_p = each(lambda bt, w: pack_diag(_mmb(bt, w)).astype(BF16), b_ht, w_t)
    n_p = each(lambda bt, kt, vt, v_: pack_diag(_mmb(jnp.concatenate([bt, kt], axis=1),
                                                      jnp.concatenate([vt, v_], axis=0))),
               b_ht, k_ht, v_t, v)
    gcol = each(lambda m: jnp.sum(jnp.where(eye2, jnp.exp(jnp.broadcast_to(m[C - 1:C, :], (L, L))), 0.0),
                                  axis=1, keepdims=True), cm)
    g_p = each(lambda g_: pack_diag(jnp.broadcast_to(g_, (L, L))), gcol)

    S = state_ref[...]
    ys = []
    for c in chunks:
        prod = _mm(jnp.concatenate([m_p[c], q[c]], axis=0), sm(S))
        ys.append(prod[N:] + y0[c])
        S = g_p[c] * S + prod[:N] + n_p[c]
    state_ref[...] = S

    y = jnp.concatenate(ys, axis=0)
    inv_n = 1.0 / N
    yc = y - head_sum(y) * inv_n
    var = head_sum(yc * yc) * inv_n
    yn = yc * lax.rsqrt(var + N * GN_EPS_PER_CH) * lng_ref[...] + lnb_ref[...]
    o_ref[...] = ((yn + bv_all) * g_all).astype(o_ref.dtype)


def rwkv_time_mix(p, mu, w0, w_up, a0, a_up, g_up, k_k, k_a, r_k, lnx_g, lnx_b,
                  batch, seq, tc, first_col):
    n = p.shape[0]
    rw = w0.shape[0]
    L = SCAN_HEADS * RWKV_HEAD_DIM
    dl, il = w_up.shape[0], a_up.shape[0]
    W = V7X_LANES
    tw = p.shape[1] - first_col - 3 * rw
    tb = (first_col + 3 * rw) // W
    assert tw == 3 * W and g_up.shape[0] == 2 * W and (first_col + 3 * rw) % W == 0
    assert SCAN_CHUNK == RWKV_HEAD_DIM
    assert dl + il == V7X_LANES and seq % tc == 0 and tc % SCAN_CHUNK == 0 and rw % L == 0 and first_col % L == 0
    nt = seq // tc
    ng = rw // L
    cb = first_col // L
    sub = V7X_SUBLANES
    row = lambda a: a.reshape(1, -1)
    mur, muk, muv, mut = mu[:rw], mu[rw:2 * rw], mu[2 * rw:3 * rw], mu[3 * rw:]
    wup_pad = jnp.concatenate([w_up, jnp.zeros((il, rw), w_up.dtype)], axis=0).astype(BF16)
    aup_pad = jnp.concatenate([jnp.zeros((dl, rw), a_up.dtype), a_up], axis=0).astype(BF16)
    gup = g_up.astype(BF16)

    blk = lambda i: pl.BlockSpec((tc, L), lambda b, g, c: (b * nt + c, cb + i * ng + g))
    tail = lambda j: pl.BlockSpec((tc, W), lambda b, g, c: (b * nt + c, tb + j))
    vec = pl.BlockSpec((1, L), lambda b, g, c: (0, g))
    lora = lambda a: pl.BlockSpec((a.shape[0], L), lambda b, g, c: (0, g))
    return pl.pallas_call(
        functools.partial(_rwkv_kernel, heads=SCAN_HEADS, nchunks=tc // SCAN_CHUNK),
        grid=(batch, ng, nt),
        in_specs=[blk(0), blk(1), blk(2), tail(0), tail(1), tail(2),
                  vec, vec, vec, pl.BlockSpec((1, tw), lambda b, g, c: (0, 0)),
                  vec, lora(wup_pad), vec, lora(aup_pad), lora(gup), vec, vec, vec, vec, vec],
        out_specs=pl.BlockSpec((tc, L), lambda b, g, c: (b * nt + c, g)),
        out_shape=jax.ShapeDtypeStruct((n, rw), BF16),
        scratch_shapes=[pltpu.VMEM((RWKV_HEAD_DIM, L), F32),
                        pltpu.VMEM((3, sub, L), F32),
                        pltpu.VMEM((3, sub, W), F32)],
        compiler_params=_cparams("parallel", "parallel", "arbitrary"),
        name="rwkv_time_mix",
    )(*([p] * 6),
      row(mur), row(muk), row(muv), row(mut), row(w0), wup_pad, row(a0), aup_pad, gup,
      row(k_k), row(k_a), row(r_k), row(lnx_g), row(lnx_b))


def _post_mix_kernel(mix_ref, x_ref, gpost_ref, gpre_ref, x1_ref, xn_ref):
    m = mix_ref[...]
    x1 = x_ref[...] + m * lax.rsqrt(jnp.mean(m * m, axis=-1, keepdims=True) + RMS_EPS) * gpost_ref[...]
    x1_ref[...] = x1
    xn_ref[...] = (x1 * lax.rsqrt(jnp.mean(x1 * x1, axis=-1, keepdims=True) + RMS_EPS)
                   * gpre_ref[...]).astype(xn_ref.dtype)


def post_mix(mix, x, g_post, g_pre, tm):
    n, d = x.shape
    blk = pl.BlockSpec((tm, d), lambda i: (i, 0))
    vec = pl.BlockSpec((1, d), lambda i: (0, 0))
    return pl.pallas_call(
        _post_mix_kernel,
        grid=(n // tm,),
        in_specs=[blk, blk, vec, vec],
        out_specs=[blk, blk],
        out_shape=[jax.ShapeDtypeStruct((n, d), F32), jax.ShapeDtypeStruct((n, d), BF16)],
        compiler_params=_cparams("parallel"),
        name="post_mix",
    )(mix, x, g_post.reshape(1, d), g_pre.reshape(1, d))


def _ffn_kernel(xn_ref, w1_ref, w2_ref, o_ref):
    @pl.when(pl.program_id(1) == 0)
    def _():
        o_ref[...] = jnp.zeros_like(o_ref)

    h = jnp.maximum(_mm(xn_ref[...], w1_ref[...]), 0.0)
    o_ref[...] += _mm((h * h).astype(BF16), w2_ref[...])


def ffn(xn, w1, w2, tm, tf):
    n, d = xn.shape
    f = w1.shape[1]
    return pl.pallas_call(
        _ffn_kernel,
        grid=(n // tm, f // tf),
        in_specs=[pl.BlockSpec((tm, d), lambda i, j: (i, 0), pipeline_mode=pl.Buffered(1)),
                  pl.BlockSpec((d, tf), lambda i, j: (0, j)),
                  pl.BlockSpec((tf, d), lambda i, j: (j, 0))],
        out_specs=pl.BlockSpec((tm, d), lambda i, j: (i, 0)),
        out_shape=jax.ShapeDtypeStruct((n, d), F32),
        compiler_params=_cparams("parallel", "arbitrary"),
        name="ffn",
    )(xn, w1, w2)


def _post_ffn_kernel(f_ref, x_ref, g_ref, o_ref):
    f = f_ref[...]
    o_ref[...] = x_ref[...] + f * lax.rsqrt(jnp.mean(f * f, axis=-1, keepdims=True) + RMS_EPS) * g_ref[...]


def post_ffn(f, x1, g, tm):
    n, d = x1.shape
    blk = pl.BlockSpec((tm, d), lambda i: (i, 0))
    return pl.pallas_call(
        _post_ffn_kernel,
        grid=(n // tm,),
        in_specs=[blk, blk, pl.BlockSpec((1, d), lambda i: (0, 0))],
        out_specs=blk,
        out_shape=jax.ShapeDtypeStruct((n, d), F32),
        compiler_params=_cparams("parallel"),
        name="post_ffn",
    )(f, x1, g.reshape(1, d))


def _tile(n, want):
    if n <= want:
        return n
    t = want - want % V7X_LANES
    while n % t:
        t -= V7X_LANES
    assert t > 0, (n, want)
    return t


def kernel(x, pre_mix_g, w_in, tshift_mu, gmlp_ln_g, gmlp_ln_b, gmlp_ws, gmlp_bs, decay_w0, decay_up, iclr_a0, iclr_up, gate_up, k_k, k_a, r_k, lnx_g, lnx_b, w_out, post_mix_g, pre_ffn_g, w_ff1, w_ff2, post_ffn_g):
    batch, seq, d = x.shape
    depth = w_in.shape[0]
    n = batch * seq
    gw = gmlp_ws.shape[1] * V7X_LANES
    rw = decay_w0.shape[1]
    xf = x.reshape(n, d)
    for l in range(depth):
        h = rms_norm_bf16(xf, pre_mix_g[l], _tile(n, 512))
        p = matmul(h, w_in[l], _tile(n, 1024), 768, "w_in")

        y_a = gmlp_gating(p, gmlp_ln_g[l], gmlp_ln_b[l], gmlp_ws[l], gmlp_bs[l], _tile(seq, 512))
        y_b = rwkv_time_mix(p, tshift_mu[l], decay_w0[l], decay_up[l], iclr_a0[l], iclr_up[l],
                            gate_up[l], k_k[l], k_a[l], r_k[l].reshape(-1), lnx_g[l], lnx_b[l],
                            batch, seq, _tile(seq, 512), first_col=2 * gw)

        mix = matmul_concat2(y_a, y_b, w_out[l].astype(BF16), _tile(n, 1024), _tile(d, 1024), "w_out")
        x1, xn = post_mix(mix, xf, post_mix_g[l], pre_ffn_g[l], _tile(n, 256))
        f = ffn(xn, w_ff1[l].astype(BF16), w_ff2[l].astype(BF16), _tile(n, 1024), _tile(w_ff1.shape[2], 512))
        xf = post_ffn(f, x1, post_ffn_g[l], _tile(n, 256))
    return xf.reshape(batch, seq, d)
```
